```python
import jax, jax.numpy as jnp
from jax import lax
import numpy as np

D_MODEL = 1024
BATCH = 4
SEQ = 4096
DEPTH = 2

CHUNK = 64
N_A_LAYERS = DEPTH // 2
N_B_LAYERS = DEPTH - N_A_LAYERS
A_WIDTH = 2 * D_MODEL
A_GROUP_LEN = 128
A_HEAD_CH = 128
A_HEADS = A_WIDTH // A_HEAD_CH
B_HEAD_DIM = 64
B_HEADS = D_MODEL // B_HEAD_DIM
B_WIDTH = B_HEADS * B_HEAD_DIM
B_PREV_CHUNKS = 8
B_BAND = (B_PREV_CHUNKS + 1) * CHUNK
B_LEFT_PAD = B_PREV_CHUNKS * CHUNK
REL_CLIP = 256
N_REL = 2 * REL_CLIP + 1
EPS = 1e-6
NEG_INF = -1e30

kernel_name = "yoco_gmlp_chunkattn_hybrid"


def rms_norm(x, g):
    xf = x.astype(jnp.float32)
    y = xf * lax.rsqrt(jnp.mean(xf * xf, axis=-1, keepdims=True) + EPS)
    return (y * g.astype(jnp.float32)).astype(x.dtype)


def layer_norm(x, g, b):
    xf = x.astype(jnp.float32)
    mu = jnp.mean(xf, axis=-1, keepdims=True)
    xc = xf - mu
    y = xc * lax.rsqrt(jnp.mean(xc * xc, axis=-1, keepdims=True) + EPS)
    return (y * g.astype(jnp.float32) + b.astype(jnp.float32)).astype(x.dtype)


def gmlp_mixer(h, w_in, ln_g, ln_b, w_s, b_s, w_out):
    bsz, s, _ = h.shape
    u, v, z = jnp.split(h @ w_in, 3, axis=-1)
    u = jax.nn.gelu(u)
    v = layer_norm(jax.nn.gelu(v), ln_g, ln_b)
    vg = v.reshape(bsz, s // A_GROUP_LEN, A_GROUP_LEN, A_HEADS, A_HEAD_CH)
    pos_chunk = jnp.arange(A_GROUP_LEN) // CHUNK
    allowed = pos_chunk[:, None] >= pos_chunk[None, :]
    w_m = jnp.where(allowed[None], w_s, jnp.zeros_like(w_s)).astype(vg.dtype)
    mixed = jnp.einsum('hts,bgshc->bgthc', w_m, vg)
    mixed = mixed + b_s.T.astype(vg.dtype)[:, :, None]
    mixed = mixed.reshape(bsz, s, A_WIDTH)
    y = u * mixed * jax.nn.silu(z)
    return y @ w_out


def shared_kv(x, g_kv, w_kv):
    bsz, s, _ = x.shape
    k, v = jnp.split(rms_norm(x, g_kv) @ w_kv, 2, axis=-1)
    pad = ((0, 0), (B_LEFT_PAD, 0), (0, 0), (0, 0))
    k = jnp.pad(k.reshape(bsz, s, B_HEADS, B_HEAD_DIM), pad)
    v = jnp.pad(v.reshape(bsz, s, B_HEADS, B_HEAD_DIM), pad)
    return k, v


def chunk_attention_mixer(h, k_pad, v_pad, w_qz, rel_bias, w_out):
    bsz, s, _ = h.shape
    n_chunks = s // CHUNK
    q, z = jnp.split(h @ w_qz, 2, axis=-1)
    scale = B_HEAD_DIM ** -0.5
    q = (q * scale).reshape(bsz, n_chunks, CHUNK, B_HEADS, B_HEAD_DIM)
    q = jnp.moveaxis(q, 1, 0)
    qi = jnp.arange(CHUNK)
    m = jnp.arange(B_BAND)
    rel = qi[:, None] - m[None, :] + B_LEFT_PAD
    idx = jnp.clip(rel, -REL_CLIP, REL_CLIP) + REL_CLIP
    bias = rel_bias[:, idx].astype(jnp.float32)

    def one_chunk(args):
        c, qc = args
        start = c * CHUNK
        kc = lax.dynamic_slice_in_dim(k_pad, start, B_BAND, axis=1)
        vc = lax.dynamic_slice_in_dim(v_pad, start, B_BAND, axis=1)
        sc = jnp.einsum('bqhd,bkhd->bhqk', qc, kc).astype(jnp.float32) + bias
        valid = (m + start) >= B_LEFT_PAD
        sc = jnp.where(valid[None, None, None, :], sc, NEG_INF)
        p = jax.nn.softmax(sc, axis=-1).astype(vc.dtype)
        return jnp.einsum('bhqk,bkhd->bqhd', p, vc)

    o = lax.map(one_chunk, (jnp.arange(n_chunks), q))
    o = jnp.moveaxis(o, 0, 1).reshape(bsz, s, B_WIDTH)
    return (o * jax.nn.silu(z)) @ w_out


def setup_inputs(seed: int = 0) -> dict:
    key = jax.random.key(seed)
    ks = jax.random.split(key, 20)
    f32 = jnp.float32
    nrm = lambda k, shp, sc: (jax.random.normal(k, shp, f32) * sc)
    x = jax.random.normal(ks[0], (BATCH, SEQ, D_MODEL), f32)
    a_norm_g = 1.0 + nrm(ks[1], (N_A_LAYERS, D_MODEL), 0.02)
    a_w_in = nrm(ks[2], (N_A_LAYERS, D_MODEL, 3 * A_WIDTH), D_MODEL ** -0.5)
    a_ln_g = 1.0 + nrm(ks[3], (N_A_LAYERS, A_WIDTH), 0.02)
    a_ln_b = nrm(ks[4], (N_A_LAYERS, A_WIDTH), 0.02)
    a_w_s = nrm(ks[5], (N_A_LAYERS, A_HEADS, A_GROUP_LEN, A_GROUP_LEN), A_GROUP_LEN ** -0.5)
    a_b_s = 1.0 + nrm(ks[6], (N_A_LAYERS, A_HEADS, A_GROUP_LEN), 0.1)
    a_w_out = nrm(ks[7], (N_A_LAYERS, A_WIDTH, D_MODEL), A_WIDTH ** -0.5)
    kv_norm_g = 1.0 + nrm(ks[8], (D_MODEL,), 0.02)
    w_kv = nrm(ks[9], (D_MODEL, 2 * B_WIDTH), D_MODEL ** -0.5)
    b_norm_g = 1.0 + nrm(ks[10], (N_B_LAYERS, D_MODEL), 0.02)
    b_w_qz = nrm(ks[11], (N_B_LAYERS, D_MODEL, 2 * B_WIDTH), D_MODEL ** -0.5)
    b_rel_bias = nrm(ks[12], (N_B_LAYERS, B_HEADS, N_REL), 0.5)
    b_w_out = nrm(ks[13], (N_B_LAYERS, B_WIDTH, D_MODEL), B_WIDTH ** -0.5)
    final_norm_g = 1.0 + nrm(ks[14], (D_MODEL,), 0.02)
    return {"x": x, "a_norm_g": a_norm_g, "a_w_in": a_w_in, "a_ln_g": a_ln_g,
            "a_ln_b": a_ln_b, "a_w_s": a_w_s, "a_b_s": a_b_s, "a_w_out": a_w_out,
            "kv_norm_g": kv_norm_g, "w_kv": w_kv, "b_norm_g": b_norm_g,
            "b_w_qz": b_w_qz, "b_rel_bias": b_rel_bias, "b_w_out": b_w_out,
            "final_norm_g": final_norm_g}


def reference(x, a_norm_g, a_w_in, a_ln_g, a_ln_b, a_w_s, a_b_s, a_w_out,
              kv_norm_g, w_kv, b_norm_g, b_w_qz, b_rel_bias, b_w_out, final_norm_g):
    k_pad = None
    v_pad = None
    for layer in range(DEPTH):
        if layer < N_A_LAYERS:
            i = layer
            h = rms_norm(x, a_norm_g[i])
            x = x + gmlp_mixer(h, a_w_in[i], a_ln_g[i], a_ln_b[i], a_w_s[i],
                               a_b_s[i], a_w_out[i])
        else:
            i = layer - N_A_LAYERS
            if i == 0:
                k_pad, v_pad = shared_kv(x, kv_norm_g, w_kv)
            h = rms_norm(x, b_norm_g[i])
            x = x + chunk_attention_mixer(h, k_pad, v_pad, b_w_qz[i],
                                          b_rel_bias[i], b_w_out[i])
    return rms_norm(x, final_norm_g)
```

```python
import functools

import jax
import jax.numpy as jnp
from jax import lax
from jax.experimental import pallas as pl
from jax.experimental.pallas import tpu as pltpu

D_MODEL = 1024
CHUNK = 64
A_WIDTH = 2 * D_MODEL
A_GROUP_LEN = 128
A_HEAD_CH = 128
A_HEADS = A_WIDTH // A_HEAD_CH
B_HEAD_DIM = 64
B_HEADS = D_MODEL // B_HEAD_DIM
B_WIDTH = B_HEADS * B_HEAD_DIM
B_PREV_CHUNKS = 8
B_BAND = (B_PREV_CHUNKS + 1) * CHUNK
B_LEFT_PAD = B_PREV_CHUNKS * CHUNK
REL_CLIP = 256
EPS = 1e-6
NEG_INF = -1e30

LANES = 128
MXU_WIDTH = 256
VMEM_LIMIT_BYTES = 56 * 1024 * 1024

TILE_A = 512
TILE_B = B_LEFT_PAD
COL = MXU_WIDTH
HEAD_PAIR = 2 * B_HEAD_DIM

BF16 = jnp.bfloat16
F32 = jnp.float32


def _dot(a, b):
    return jnp.dot(a, b, preferred_element_type=F32)


def _dot_nt(a, b):
    return lax.dot_general(a, b, (((1,), (1,)), ((), ())), preferred_element_type=F32)


def _lane_fold(v):
    acc = v[:, :LANES]
    for i in range(1, v.shape[1] // LANES):
        acc = acc + v[:, i * LANES:(i + 1) * LANES]
    return acc


def _layer_a_kernel(x_ref, g_ref, win_ref, lng_ref, lnb_ref, ws_ref, bst_ref, wout_ref,
                    o_ref, h_s, gv_s, vn_s, y_s):
    n_col = A_WIDTH // COL
    n_grp = TILE_A // A_GROUP_LEN

    x = x_ref[...]
    ms = jnp.mean(x * x, axis=-1, keepdims=True)
    h_s[...] = (x * lax.rsqrt(ms + EPS) * g_ref[...]).astype(BF16)

    acc = jnp.zeros((TILE_A, LANES), F32)
    for c in range(n_col):
        cs = slice(c * COL, (c + 1) * COL)
        gv = jax.nn.gelu(_dot(h_s[...], win_ref[:, A_WIDTH + c * COL:A_WIDTH + (c + 1) * COL]))
        gv_s[:, cs] = gv
        acc = acc + _lane_fold(gv)
    mu = jnp.sum(acc, axis=-1, keepdims=True) * (1.0 / A_WIDTH)
    acc = jnp.zeros((TILE_A, LANES), F32)
    for c in range(n_col):
        d = gv_s[:, c * COL:(c + 1) * COL] - mu
        acc = acc + _lane_fold(d * d)
    rstd = lax.rsqrt(jnp.sum(acc, axis=-1, keepdims=True) * (1.0 / A_WIDTH) + EPS)
    for c in range(n_col):
        cs = slice(c * COL, (c + 1) * COL)
        vn = (gv_s[:, cs] - mu) * rstd * lng_ref[:, cs] + lnb_ref[:, cs]
        vn_s[:, cs] = vn.astype(BF16)

    t_idx = lax.broadcasted_iota(jnp.int32, (A_GROUP_LEN, A_GROUP_LEN), 0) // CHUNK
    s_idx = lax.broadcasted_iota(jnp.int32, (A_GROUP_LEN, A_GROUP_LEN), 1) // CHUNK
    allowed = t_idx >= s_idx

    for c in range(n_col):
        u = jax.nn.gelu(_dot(h_s[...], win_ref[:, c * COL:(c + 1) * COL]))
        sz = jax.nn.silu(_dot(h_s[...], win_ref[:, 2 * A_WIDTH + c * COL:2 * A_WIDTH + (c + 1) * COL]))
        for hh in range(COL // A_HEAD_CH):
            hd = c * (COL // A_HEAD_CH) + hh
            hs = slice(hd * A_HEAD_CH, (hd + 1) * A_HEAD_CH)
            ls = slice(hh * A_HEAD_CH, (hh + 1) * A_HEAD_CH)
            wm = jnp.where(allowed, ws_ref[hd], 0.0).astype(BF16)
            bias = jnp.broadcast_to(bst_ref[:, hd:hd + 1], (A_GROUP_LEN, A_HEAD_CH))
            for g in range(n_grp):
                rs = slice(g * A_GROUP_LEN, (g + 1) * A_GROUP_LEN)
                mixed = _dot(wm, vn_s[rs, hs]) + bias
                y_s[rs, hs] = (u[rs, ls] * mixed * sz[rs, ls]).astype(BF16)

    o_ref[...] = x_ref[...] + _dot(y_s[...], wout_ref[...])


def _layer_a(x2d, g, w_in, ln_g, ln_b, w_s, b_st, w_out):
    n_tok = x2d.shape[0]
    const = lambda shape: pl.BlockSpec(shape, lambda i: (0,) * len(shape),
                                       pipeline_mode=pl.Buffered(1))
    return pl.pallas_call(
        _layer_a_kernel,
        grid=(n_tok // TILE_A,),
        in_specs=[
            pl.BlockSpec((TILE_A, D_MODEL), lambda i: (i, 0)),
            const((1, D_MODEL)),
            const((D_MODEL, 3 * A_WIDTH)),
            const((1, A_WIDTH)),
            const((1, A_WIDTH)),
            const((A_HEADS, A_GROUP_LEN, A_GROUP_LEN)),
            const((A_GROUP_LEN, A_HEADS)),
            const((A_WIDTH, D_MODEL)),
        ],
        out_specs=pl.BlockSpec((TILE_A, D_MODEL), lambda i: (i, 0)),
        out_shape=jax.ShapeDtypeStruct((n_tok, D_MODEL), F32),
        scratch_shapes=[
            pltpu.VMEM((TILE_A, D_MODEL), BF16),
            pltpu.VMEM((TILE_A, A_WIDTH), F32),
            pltpu.VMEM((TILE_A, A_WIDTH), BF16),
            pltpu.VMEM((TILE_A, A_WIDTH), BF16),
        ],
        compiler_params=pltpu.CompilerParams(
            dimension_semantics=("arbitrary",), vmem_limit_bytes=VMEM_LIMIT_BYTES),
        name="layer_a_gmlp",
    )(x2d, g, w_in, ln_g, ln_b, w_s, b_st, w_out)


def _layer_b_kernel(x_ref, gkv_ref, gb_ref, wkv_ref, wqz_ref, bias_ref, wout_ref, gf_ref,
                    o_ref, hk_s, hq_s, k_s, v_s, q_s, sz_s, og_s):
    t = pl.program_id(1)
    n_col = B_WIDTH // COL
    n_pair = B_WIDTH // HEAD_PAIR
    cur = slice(B_LEFT_PAD, B_LEFT_PAD + TILE_B)

    @pl.when(t == 0)
    def _():
        k_s[0:B_LEFT_PAD, :] = jnp.zeros((B_LEFT_PAD, B_WIDTH), BF16)
        v_s[0:B_LEFT_PAD, :] = jnp.zeros((B_LEFT_PAD, B_WIDTH), BF16)

    x = x_ref[...]
    ms = jnp.mean(x * x, axis=-1, keepdims=True)
    xn = x * lax.rsqrt(ms + EPS)
    hk_s[...] = (xn * gkv_ref[...]).astype(BF16)
    hq_s[...] = (xn * gb_ref[...]).astype(BF16)

    scale = B_HEAD_DIM ** -0.5
    for c in range(n_col):
        cs = slice(c * COL, (c + 1) * COL)
        k_s[cur, cs] = _dot(hk_s[...], wkv_ref[:, c * COL:(c + 1) * COL]).astype(BF16)
        v_s[cur, cs] = _dot(hk_s[...], wkv_ref[:, B_WIDTH + c * COL:B_WIDTH + (c + 1) * COL]).astype(BF16)
        q_s[:, cs] = (_dot(hq_s[...], wqz_ref[:, c * COL:(c + 1) * COL]) * scale).astype(BF16)
        sz_s[:, cs] = jax.nn.silu(_dot(hq_s[...], wqz_ref[:, B_WIDTH + c * COL:B_WIDTH + (c + 1) * COL]))

    lane = lax.broadcasted_iota(jnp.int32, (CHUNK, HEAD_PAIR), 1)
    first_head = lane < B_HEAD_DIM
    band_pos = lax.broadcasted_iota(jnp.int32, (1, B_BAND), 1)

    def chunk_body(c, carry):
        r0 = pl.multiple_of(c * CHUNK, CHUNK)
        valid = jnp.logical_or(t > 0, band_pos + r0 >= B_LEFT_PAD)
        for p in range(n_pair):
            ps = slice(p * HEAD_PAIR, (p + 1) * HEAD_PAIR)
            qp = q_s[pl.ds(r0, CHUNK), ps]
            zero = jnp.zeros_like(qp)
            q2 = jnp.concatenate([jnp.where(first_head, qp, zero),
                                  jnp.where(first_head, zero, qp)], axis=0)
            kb = k_s[pl.ds(r0, B_BAND), ps]
            vb = v_s[pl.ds(r0, B_BAND), ps]
            s = _dot_nt(q2, kb) + bias_ref[p]
            s = jnp.where(valid, s, NEG_INF)
            m = jnp.max(s, axis=-1, keepdims=True)
            e = jnp.exp(s - m)
            inv_l = 1.0 / jnp.sum(e, axis=-1, keepdims=True)
            o2 = _dot(e.astype(BF16), vb) * inv_l
            o = jnp.where(first_head, o2[:CHUNK], o2[CHUNK:])
            og_s[pl.ds(r0, CHUNK), ps] = (o * sz_s[pl.ds(r0, CHUNK), ps]).astype(BF16)
        return carry

    lax.fori_loop(0, TILE_B // CHUNK, chunk_body, 0)

    y = x_ref[...] + _dot(og_s[...], wout_ref[...])
    ms = jnp.mean(y * y, axis=-1, keepdims=True)
    o_ref[...] = y * lax.rsqrt(ms + EPS) * gf_ref[...]

    k_s[0:B_LEFT_PAD, :] = k_s[cur, :]
    v_s[0:B_LEFT_PAD, :] = v_s[cur, :]


def _layer_b(x2d, batch, g_kv, g_b, w_kv, w_qz, bias2, w_out, g_f):
    n_tok = x2d.shape[0]
    tiles = n_tok // batch // TILE_B
    const = lambda shape: pl.BlockSpec(shape, lambda b, t: (0,) * len(shape),
                                       pipeline_mode=pl.Buffered(1))
    return pl.pallas_call(
        _layer_b_kernel,
        grid=(batch, tiles),
        in_specs=[
            pl.BlockSpec((TILE_B, D_MODEL), lambda b, t: (b * tiles + t, 0)),
            const((1, D_MODEL)),
            const((1, D_MODEL)),
            const((D_MODEL, 2 * B_WIDTH)),
            const((D_MODEL, 2 * B_WIDTH)),
            const((B_HEADS // 2, 2 * CHUNK, B_BAND)),
            const((B_WIDTH, D_MODEL)),
            const((1, D_MODEL)),
        ],
        out_specs=pl.BlockSpec((TILE_B, D_MODEL), lambda b, t: (b * tiles + t, 0)),
        out_shape=jax.ShapeDtypeStruct((n_tok, D_MODEL), F32),
        scratch_shapes=[
            pltpu.VMEM((TILE_B, D_MODEL), BF16),
            pltpu.VMEM((TILE_B, D_MODEL), BF16),
            pltpu.VMEM((B_LEFT_PAD + TILE_B, B_WIDTH), BF16),
            pltpu.VMEM((B_LEFT_PAD + TILE_B, B_WIDTH), BF16),
            pltpu.VMEM((TILE_B, B_WIDTH), BF16),
            pltpu.VMEM((TILE_B, B_WIDTH), F32),
            pltpu.VMEM((TILE_B, B_WIDTH), BF16),
        ],
        compiler_params=pltpu.CompilerParams(
            dimension_semantics=("arbitrary", "arbitrary"), vmem_limit_bytes=VMEM_LIMIT_BYTES),
        name="layer_b_attention",
    )(x2d, g_kv, g_b, w_kv, w_qz, bias2, w_out, g_f)


def _rel_bias_table(rel_bias):
    qi = jnp.arange(CHUNK)
    m = jnp.arange(B_BAND)
    rel = qi[:, None] - m[None, :] + B_LEFT_PAD
    idx = jnp.clip(rel, -REL_CLIP, REL_CLIP) + REL_CLIP
    bias = rel_bias[:, idx].astype(F32)
    return bias.reshape(B_HEADS // 2, 2 * CHUNK, B_BAND)


def kernel(x, a_norm_g, a_w_in, a_ln_g, a_ln_b, a_w_s, a_b_s, a_w_out, kv_norm_g, w_kv,
           b_norm_g, b_w_qz, b_rel_bias, b_w_out, final_norm_g):
    batch, seq, d = x.shape
    assert d == D_MODEL and seq % TILE_B == 0 and (batch * seq) % TILE_A == 0
    assert a_w_in.shape[0] == 1 and b_w_qz.shape[0] == 1
    x2d = x.reshape(batch * seq, d)
    row = lambda v: v.reshape(1, -1).astype(F32)

    x1 = _layer_a(x2d, row(a_norm_g[0]), a_w_in[0].astype(BF16), row(a_ln_g[0]), row(a_ln_b[0]),
                  a_w_s[0], a_b_s[0].T, a_w_out[0].astype(BF16))
    out = _layer_b(x1, batch, row(kv_norm_g), row(b_norm_g[0]), w_kv.astype(BF16),
                   b_w_qz[0].astype(BF16), _rel_bias_table(b_rel_bias[0]),
                   b_w_out[0].astype(BF16), row(final_norm_g))
    return out.reshape(batch, seq, d)
```

```python
import jax
import jax.numpy as jnp
from jax import lax
from jax.experimental import pallas as pl
from jax.experimental.pallas import tpu as pltpu

D_MODEL = 1024
CHUNK = 64
A_WIDTH = 2 * D_MODEL
A_GROUP_LEN = 128
A_HEAD_CH = 128
A_HEADS = A_WIDTH // A_HEAD_CH
B_HEAD_DIM = 64
B_HEADS = D_MODEL // B_HEAD_DIM
B_WIDTH = B_HEADS * B_HEAD_DIM
B_PREV_CHUNKS = 8
B_LEFT_PAD = B_PREV_CHUNKS * CHUNK
REL_CLIP = 256
EPS = 1e-6
NEG_INF = -1e30

LANES = 128
SUBLANES = 8
MXU_WIDTH = 256
VMEM_LIMIT_BYTES = 56 * 1024 * 1024

TILE_A = 512
TILE_B = B_LEFT_PAD
COL = MXU_WIDTH

BF16 = jnp.bfloat16
F32 = jnp.float32


def _dot(a, b):
    return jnp.dot(a, b, preferred_element_type=F32)


def _dot_nt(a, b):
    return lax.dot_general(a, b, (((1,), (1,)), ((), ())), preferred_element_type=F32)


def _lane_fold(v):
    acc = v[:, :LANES]
    for i in range(1, v.shape[1] // LANES):
        acc = acc + v[:, i * LANES:(i + 1) * LANES]
    return acc


def _layer_a_kernel(x_ref, g_ref, win_ref, lng_ref, lnb_ref, ws_ref, bst_ref, wout_ref,
                    o_ref, h_s, gv_s, vn_s, y_s):
    n_col = A_WIDTH // COL
    n_grp = TILE_A // A_GROUP_LEN

    x = x_ref[...]
    ms = jnp.mean(x * x, axis=-1, keepdims=True)
    h_s[...] = (x * lax.rsqrt(ms + EPS) * g_ref[...]).astype(BF16)

    acc = jnp.zeros((TILE_A, LANES), F32)
    for c in range(n_col):
        cs = slice(c * COL, (c + 1) * COL)
        gv = jax.nn.gelu(_dot(h_s[...], win_ref[:, A_WIDTH + c * COL:A_WIDTH + (c + 1) * COL]))
        gv_s[:, cs] = gv
        acc = acc + _lane_fold(gv)
    mu = jnp.sum(acc, axis=-1, keepdims=True) * (1.0 / A_WIDTH)
    acc = jnp.zeros((TILE_A, LANES), F32)
    for c in range(n_col):
        d = gv_s[:, c * COL:(c + 1) * COL] - mu
        acc = acc + _lane_fold(d * d)
    rstd = lax.rsqrt(jnp.sum(acc, axis=-1, keepdims=True) * (1.0 / A_WIDTH) + EPS)
    for c in range(n_col):
        cs = slice(c * COL, (c + 1) * COL)
        vn = (gv_s[:, cs] - mu) * rstd * lng_ref[:, cs] + lnb_ref[:, cs]
        vn_s[:, cs] = vn.astype(BF16)

    t_idx = lax.broadcasted_iota(jnp.int32, (A_GROUP_LEN, A_GROUP_LEN), 0) // CHUNK
    s_idx = lax.broadcasted_iota(jnp.int32, (A_GROUP_LEN, A_GROUP_LEN), 1) // CHUNK
    allowed = t_idx >= s_idx

    for c in range(n_col):
        u = jax.nn.gelu(_dot(h_s[...], win_ref[:, c * COL:(c + 1) * COL]))
        sz = jax.nn.silu(_dot(h_s[...], win_ref[:, 2 * A_WIDTH + c * COL:2 * A_WIDTH + (c + 1) * COL]))
        for hh in range(COL // A_HEAD_CH):
            hd = c * (COL // A_HEAD_CH) + hh
            hs = slice(hd * A_HEAD_CH, (hd + 1) * A_HEAD_CH)
            ls = slice(hh * A_HEAD_CH, (hh + 1) * A_HEAD_CH)
            wm = jnp.where(allowed, ws_ref[hd], 0.0).astype(BF16)
            bias = jnp.broadcast_to(bst_ref[:, hd:hd + 1], (A_GROUP_LEN, A_HEAD_CH))
            for g in range(n_grp):
                rs = slice(g * A_GROUP_LEN, (g + 1) * A_GROUP_LEN)
                mixed = _dot(wm, vn_s[rs, hs]) + bias
                y_s[rs, hs] = (u[rs, ls] * mixed * sz[rs, ls]).astype(BF16)

    o_ref[...] = x_ref[...] + _dot(y_s[...], wout_ref[...])


def _layer_a(x2d, g, w_in, ln_g, ln_b, w_s, b_st, w_out):
    n_tok = x2d.shape[0]
    const = lambda shape: pl.BlockSpec(shape, lambda i: (0,) * len(shape),
                                       pipeline_mode=pl.Buffered(1))
    return pl.pallas_call(
        _layer_a_kernel,
        grid=(n_tok // TILE_A,),
        in_specs=[
            pl.BlockSpec((TILE_A, D_MODEL), lambda i: (i, 0)),
            const((1, D_MODEL)),
            const((D_MODEL, 3 * A_WIDTH)),
            const((1, A_WIDTH)),
            const((1, A_WIDTH)),
            const((A_HEADS, A_GROUP_LEN, A_GROUP_LEN)),
            const((A_GROUP_LEN, A_HEADS)),
            const((A_WIDTH, D_MODEL)),
        ],
        out_specs=pl.BlockSpec((TILE_A, D_MODEL), lambda i: (i, 0)),
        out_shape=jax.ShapeDtypeStruct((n_tok, D_MODEL), F32),
        scratch_shapes=[
            pltpu.VMEM((TILE_A, D_MODEL), BF16),
            pltpu.VMEM((TILE_A, A_WIDTH), F32),
            pltpu.VMEM((TILE_A, A_WIDTH), BF16),
            pltpu.VMEM((TILE_A, A_WIDTH), BF16),
        ],
        compiler_params=pltpu.CompilerParams(
            dimension_semantics=("arbitrary",), vmem_limit_bytes=VMEM_LIMIT_BYTES),
        name="layer_a_gmlp",
    )(x2d, g, w_in, ln_g, ln_b, w_s, b_st, w_out)


B_PAIR = 2 * CHUNK
B_WIN = B_LEFT_PAD + B_PAIR
GROUP_HEADS = MXU_WIDTH // B_HEAD_DIM
GROUP_W = GROUP_HEADS * B_HEAD_DIM
N_GROUP = B_HEADS // GROUP_HEADS
GROUP_COLS = GROUP_HEADS * B_PAIR
EXT_LEN = 1024


def _fold_rows(v, op):
    parts = [v[r:r + SUBLANES] for r in range(0, v.shape[0], SUBLANES)]
    while len(parts) > 1:
        nxt = [op(parts[k], parts[k + 1]) for k in range(0, len(parts) - 1, 2)]
        if len(parts) % 2:
            nxt.append(parts[-1])
        parts = nxt
    return parts[0]


def _build_bias_table(ext_ref, bias_s):
    r_in = lax.broadcasted_iota(jnp.int32, (LANES, LANES), 0)
    c_in = lax.broadcasted_iota(jnp.int32, (LANES, LANES), 1)
    parity = c_in // CHUNK
    for jb in range(B_WIN // LANES):
        key_chunk = 2 * jb + r_in // CHUNK
        in_band = jnp.logical_and(key_chunk >= parity, key_chunk <= parity + B_PREV_CHUNKS)
        start = B_WIN - LANES * jb
        for h in range(B_HEADS):
            g, a = divmod(h, GROUP_HEADS)
            row = jnp.broadcast_to(ext_ref[h:h + 1, start:start + 2 * LANES], (LANES, 2 * LANES))
            toep = pltpu.roll(row, 0, 1, stride=1, stride_axis=0)[:, LANES:]
            bias_s[g, jb * LANES:(jb + 1) * LANES, a * LANES:(a + 1) * LANES] = jnp.where(
                in_band, toep, NEG_INF)


def _layer_b_kernel(x_ref, gkv_ref, gb_ref, wk_ref, wvt_ref, wqt_ref, wzt_ref, ext_ref, wout_ref,
                    gf_ref, o_ref, hk_s, hq_s, k_s, vt_s, qt_s, szt_s, ogt_s, bias_s):
    b = pl.program_id(0)
    t = pl.program_id(1)
    cur = slice(B_LEFT_PAD, B_LEFT_PAD + TILE_B)

    @pl.when(jnp.logical_and(b == 0, t == 0))
    def _():
        _build_bias_table(ext_ref, bias_s)

    @pl.when(t == 0)
    def _():
        k_s[0:B_LEFT_PAD, :] = jnp.zeros((B_LEFT_PAD, B_WIDTH), BF16)
        vt_s[:, 0:B_LEFT_PAD] = jnp.zeros((B_WIDTH, B_LEFT_PAD), BF16)

    x = x_ref[...]
    ms = jnp.mean(x * x, axis=-1, keepdims=True)
    xn = x * lax.rsqrt(ms + EPS)
    hk_s[...] = (xn * gkv_ref[...]).astype(BF16)
    hq_s[...] = (xn * gb_ref[...]).astype(BF16)

    scale = B_HEAD_DIM ** -0.5
    k_s[cur, :] = _dot(hk_s[...], wk_ref[...]).astype(BF16)
    vt_s[:, cur] = _dot_nt(wvt_ref[...], hk_s[...]).astype(BF16)
    qt_s[...] = (_dot_nt(wqt_ref[...], hq_s[...]) * scale).astype(BF16)
    szt_s[...] = jax.nn.silu(_dot_nt(wzt_ref[...], hq_s[...]))

    pen = jnp.where(t == 0, NEG_INF, 0.0).astype(F32)
    row_head = lax.broadcasted_iota(jnp.int32, (GROUP_W, B_PAIR), 0) // B_HEAD_DIM
    n_blk = B_WIN // LANES
    hist_blk = B_LEFT_PAD // LANES

    for i in range(TILE_B // B_PAIR):
        ts = slice(i * B_PAIR, (i + 1) * B_PAIR)
        ws = slice(i * B_PAIR, i * B_PAIR + B_WIN)
        for g in range(N_GROUP):
            gs = slice(g * GROUP_W, (g + 1) * GROUP_W)
            qt = qt_s[gs, ts]
            zero = jnp.zeros_like(qt)
            q4 = jnp.concatenate([jnp.where(row_head == a, qt, zero) for a in range(GROUP_HEADS)],
                                 axis=1)
            s = _dot(k_s[ws, gs], q4) + bias_s[g]
            blocks = [s[j * LANES:(j + 1) * LANES] for j in range(n_blk)]
            before_start = [i + j < hist_blk for j in range(n_blk)]
            m8 = None
            for j in range(n_blk):
                bm = _fold_rows(blocks[j], jnp.maximum)
                if before_start[j]:
                    bm = bm + pen
                m8 = bm if m8 is None else jnp.maximum(m8, bm)
            m = jnp.max(m8, axis=0, keepdims=True)
            m_pen = m - pen
            l8 = None
            probs = []
            for j in range(n_blk):
                e = jnp.exp(blocks[j] - (m_pen if before_start[j] else m))
                le = _fold_rows(e, jnp.add)
                l8 = le if l8 is None else l8 + le
                probs.append(e.astype(BF16))
            inv_l = 1.0 / jnp.sum(l8, axis=0, keepdims=True)
            ot = _dot(vt_s[gs, ws], jnp.concatenate(probs, axis=0))
            for a in range(GROUP_HEADS):
                rs = slice(g * GROUP_W + a * B_HEAD_DIM, g * GROUP_W + (a + 1) * B_HEAD_DIM)
                cs = slice(a * B_PAIR, (a + 1) * B_PAIR)
                o = ot[a * B_HEAD_DIM:(a + 1) * B_HEAD_DIM, cs] * inv_l[:, cs]
                ogt_s[rs, ts] = (o * szt_s[rs, ts]).astype(BF16)

    y = x_ref[...] + _dot(ogt_s[...].T, wout_ref[...])
    ms = jnp.mean(y * y, axis=-1, keepdims=True)
    o_ref[...] = y * lax.rsqrt(ms + EPS) * gf_ref[...]

    k_s[0:B_LEFT_PAD, :] = k_s[cur, :]
    vt_s[:, 0:B_LEFT_PAD] = vt_s[:, cur]


def _layer_b(x2d, batch, g_kv, g_b, w_k, w_vt, w_qt, w_zt, ext, w_out, g_f):
    n_tok = x2d.shape[0]
    tiles = n_tok // batch // TILE_B
    const = lambda shape: pl.BlockSpec(shape, lambda b, t: (0,) * len(shape),
                                       pipeline_mode=pl.Buffered(1))
    return pl.pallas_call(
        _layer_b_kernel,
        grid=(batch, tiles),
        in_specs=[
            pl.BlockSpec((TILE_B, D_MODEL), lambda b, t: (b * tiles + t, 0)),
            const((1, D_MODEL)),
            const((1, D_MODEL)),
            const((D_MODEL, B_WIDTH)),
            const((B_WIDTH, D_MODEL)),
            const((B_WIDTH, D_MODEL)),
            const((B_WIDTH, D_MODEL)),
            const((B_HEADS, EXT_LEN)),
            const((B_WIDTH, D_MODEL)),
            const((1, D_MODEL)),
        ],
        out_specs=pl.BlockSpec((TILE_B, D_MODEL), lambda b, t: (b * tiles + t, 0)),
        out_shape=jax.ShapeDtypeStruct((n_tok, D_MODEL), F32),
        scratch_shapes=[
            pltpu.VMEM((TILE_B, D_MODEL), BF16),
            pltpu.VMEM((TILE_B, D_MODEL), BF16),
            pltpu.VMEM((B_LEFT_PAD + TILE_B, B_WIDTH), BF16),
            pltpu.VMEM((B_WIDTH, B_LEFT_PAD + TILE_B), BF16),
            pltpu.VMEM((B_WIDTH, TILE_B), BF16),
            pltpu.VMEM((B_WIDTH, TILE_B), F32),
            pltpu.VMEM((B_WIDTH, TILE_B), BF16),
            pltpu.VMEM((N_GROUP, B_WIN, GROUP_COLS), F32),
        ],
        compiler_params=pltpu.CompilerParams(
            dimension_semantics=("arbitrary", "arbitrary"), vmem_limit_bytes=VMEM_LIMIT_BYTES),
        name="layer_b_attention",
    )(x2d, g_kv, g_b, w_k, w_vt, w_qt, w_zt, ext, w_out, g_f)


def kernel(x, a_norm_g, a_w_in, a_ln_g, a_ln_b, a_w_s, a_b_s, a_w_out, kv_norm_g, w_kv,
           b_norm_g, b_w_qz, b_rel_bias, b_w_out, final_norm_g):
    batch, seq, d = x.shape
    assert d == D_MODEL and seq % TILE_B == 0 and (batch * seq) % TILE_A == 0
    assert a_w_in.shape[0] == 1 and b_w_qz.shape[0] == 1
    assert b_rel_bias.shape[-1] == 2 * REL_CLIP + 1
    x2d = x.reshape(batch * seq, d)
    row = lambda v: v.reshape(1, -1).astype(F32)

    x1 = _layer_a(x2d, row(a_norm_g[0]), a_w_in[0].astype(BF16), row(a_ln_g[0]), row(a_ln_b[0]),
                  a_w_s[0], a_b_s[0].T, a_w_out[0].astype(BF16))
    w_qz = b_w_qz[0]
    ext = jnp.pad(b_rel_bias[0].astype(F32), ((0, 0), (0, EXT_LEN - b_rel_bias.shape[-1])),
                  mode="edge")
    out = _layer_b(x1, batch, row(kv_norm_g), row(b_norm_g[0]),
                   w_kv[:, :B_WIDTH].astype(BF16), w_kv[:, B_WIDTH:].T.astype(BF16),
                   w_qz[:, :B_WIDTH].T.astype(BF16), w_qz[:, B_WIDTH:].T.astype(BF16),
                   ext, b_w_out[0].astype(BF16), row(final_norm_g))
    return out.reshape(batch, seq, d)
```

```python
import jax
import jax.numpy as jnp
from jax import lax
from jax.experimental import pallas as pl
from jax.experimental.pallas import tpu as pltpu

D_MODEL = 1024
CHUNK = 64
A_WIDTH = 2 * D_MODEL
A_GROUP_LEN = 128
A_HEAD_CH = 128
A_HEADS = A_WIDTH // A_HEAD_CH
B_HEAD_DIM = 64
B_HEADS = D_MODEL // B_HEAD_DIM
B_WIDTH = B_HEADS * B_HEAD_DIM
B_PREV_CHUNKS = 8
B_LEFT_PAD = B_PREV_CHUNKS * CHUNK
REL_CLIP = 256
EPS = 1e-6
NEG_INF = -1e30

LANES = 128
SUBLANES = 8
MXU_WIDTH = 256
VMEM_LIMIT_BYTES = 56 * 1024 * 1024

TILE_A = 512
TILE_B = B_LEFT_PAD
COL = MXU_WIDTH

BF16 = jnp.bfloat16
F32 = jnp.float32


def _dot(a, b):
    return jnp.dot(a, b, preferred_element_type=F32)


def _dot_nt(a, b):
    return lax.dot_general(a, b, (((1,), (1,)), ((), ())), preferred_element_type=F32)


def _lane_fold(v):
    acc = v[:, :LANES]
    for i in range(1, v.shape[1] // LANES):
        acc = acc + v[:, i * LANES:(i + 1) * LANES]
    return acc


def _layer_a_kernel(x_ref, g_ref, win_ref, lng_ref, lnb_ref, ws_ref, bst_ref, wout_ref,
                    o_ref, h_s, gv_s, vn_s, y_s):
    n_col = A_WIDTH // COL
    n_grp = TILE_A // A_GROUP_LEN

    x = x_ref[...]
    ms = jnp.mean(x * x, axis=-1, keepdims=True)
    h_s[...] = (x * lax.rsqrt(ms + EPS) * g_ref[...]).astype(BF16)

    acc = jnp.zeros((TILE_A, LANES), F32)
    for c in range(n_col):
        cs = slice(c * COL, (c + 1) * COL)
        gv = jax.nn.gelu(_dot(h_s[...], win_ref[:, A_WIDTH + c * COL:A_WIDTH + (c + 1) * COL]))
        gv_s[:, cs] = gv
        acc = acc + _lane_fold(gv)
    mu = jnp.sum(acc, axis=-1, keepdims=True) * (1.0 / A_WIDTH)
    acc = jnp.zeros((TILE_A, LANES), F32)
    for c in range(n_col):
        d = gv_s[:, c * COL:(c + 1) * COL] - mu
        acc = acc + _lane_fold(d * d)
    rstd = lax.rsqrt(jnp.sum(acc, axis=-1, keepdims=True) * (1.0 / A_WIDTH) + EPS)
    for c in range(n_col):
        cs = slice(c * COL, (c + 1) * COL)
        vn = (gv_s[:, cs] - mu) * rstd * lng_ref[:, cs] + lnb_ref[:, cs]
        vn_s[:, cs] = vn.astype(BF16)

    t_idx = lax.broadcasted_iota(jnp.int32, (A_GROUP_LEN, A_GROUP_LEN), 0) // CHUNK
    s_idx = lax.broadcasted_iota(jnp.int32, (A_GROUP_LEN, A_GROUP_LEN), 1) // CHUNK
    allowed = t_idx >= s_idx

    for c in range(n_col):
        u = jax.nn.gelu(_dot(h_s[...], win_ref[:, c * COL:(c + 1) * COL]))
        sz = jax.nn.silu(_dot(h_s[...], win_ref[:, 2 * A_WIDTH + c * COL:2 * A_WIDTH + (c + 1) * COL]))
        for hh in range(COL // A_HEAD_CH):
            hd = c * (COL // A_HEAD_CH) + hh
            hs = slice(hd * A_HEAD_CH, (hd + 1) * A_HEAD_CH)
            ls = slice(hh * A_HEAD_CH, (hh + 1) * A_HEAD_CH)
            wm = jnp.where(allowed, ws_ref[hd], 0.0).astype(BF16)
            bias = jnp.broadcast_to(bst_ref[:, hd:hd + 1], (A_GROUP_LEN, A_HEAD_CH))
            for g in range(n_grp):
                rs = slice(g * A_GROUP_LEN, (g + 1) * A_GROUP_LEN)
                mixed = _dot(wm, vn_s[rs, hs]) + bias
                y_s[rs, hs] = (u[rs, ls] * mixed * sz[rs, ls]).astype(BF16)

    o_ref[...] = x_ref[...] + _dot(y_s[...], wout_ref[...])


def _layer_a(x2d, g, w_in, ln_g, ln_b, w_s, b_st, w_out):
    n_tok = x2d.shape[0]
    const = lambda shape: pl.BlockSpec(shape, lambda i: (0,) * len(shape),
                                       pipeline_mode=pl.Buffered(1))
    return pl.pallas_call(
        _layer_a_kernel,
        grid=(n_tok // TILE_A,),
        in_specs=[
            pl.BlockSpec((TILE_A, D_MODEL), lambda i: (i, 0)),
            const((1, D_MODEL)),
            const((D_MODEL, 3 * A_WIDTH)),
            const((1, A_WIDTH)),
            const((1, A_WIDTH)),
            const((A_HEADS, A_GROUP_LEN, A_GROUP_LEN)),
            const((A_GROUP_LEN, A_HEADS)),
            const((A_WIDTH, D_MODEL)),
        ],
        out_specs=pl.BlockSpec((TILE_A, D_MODEL), lambda i: (i, 0)),
        out_shape=jax.ShapeDtypeStruct((n_tok, D_MODEL), F32),
        scratch_shapes=[
            pltpu.VMEM((TILE_A, D_MODEL), BF16),
            pltpu.VMEM((TILE_A, A_WIDTH), F32),
            pltpu.VMEM((TILE_A, A_WIDTH), BF16),
            pltpu.VMEM((TILE_A, A_WIDTH), BF16),
        ],
        compiler_params=pltpu.CompilerParams(
            dimension_semantics=("arbitrary",), vmem_limit_bytes=VMEM_LIMIT_BYTES),
        name="layer_a_gmlp",
    )(x2d, g, w_in, ln_g, ln_b, w_s, b_st, w_out)


B_PAIR = 2 * CHUNK
B_WIN = B_LEFT_PAD + B_PAIR
GROUP_HEADS = MXU_WIDTH // B_HEAD_DIM
GROUP_W = GROUP_HEADS * B_HEAD_DIM
N_GROUP = B_HEADS // GROUP_HEADS
GROUP_COLS = GROUP_HEADS * B_PAIR
EXT_LEN = 1024


def _fold_rows(v, op):
    parts = [v[r:r + SUBLANES] for r in range(0, v.shape[0], SUBLANES)]
    while len(parts) > 1:
        nxt = [op(parts[k], parts[k + 1]) for k in range(0, len(parts) - 1, 2)]
        if len(parts) % 2:
            nxt.append(parts[-1])
        parts = nxt
    return parts[0]


def _build_bias_table(ext_ref, bias_s):
    r_in = lax.broadcasted_iota(jnp.int32, (LANES, LANES), 0)
    c_in = lax.broadcasted_iota(jnp.int32, (LANES, LANES), 1)
    parity = c_in // CHUNK
    for jb in range(B_WIN // LANES):
        key_chunk = 2 * jb + r_in // CHUNK
        in_band = jnp.logical_and(key_chunk >= parity, key_chunk <= parity + B_PREV_CHUNKS)
        start = B_WIN - LANES * jb
        for h in range(B_HEADS):
            g, a = divmod(h, GROUP_HEADS)
            row = jnp.broadcast_to(ext_ref[h:h + 1, start:start + 2 * LANES], (LANES, 2 * LANES))
            toep = pltpu.roll(row, 0, 1, stride=1, stride_axis=0)[:, LANES:]
            bias_s[g, jb * LANES:(jb + 1) * LANES, a * LANES:(a + 1) * LANES] = jnp.where(
                in_band, toep, NEG_INF)


def _layer_b_kernel(x_ref, gkv_ref, gb_ref, wk_ref, wvt_ref, wqt_ref, wzt_ref, ext_ref, wout_ref,
                    gf_ref, o_ref, hk_s, hq_s, k_s, vt_s, qt_s, szt_s, ogt_s, bias_s):
    b = pl.program_id(0)
    t = pl.program_id(1)
    cur = slice(B_LEFT_PAD, B_LEFT_PAD + TILE_B)

    @pl.when(jnp.logical_and(b == 0, t == 0))
    def _():
        _build_bias_table(ext_ref, bias_s)

    @pl.when(t == 0)
    def _():
        k_s[0:B_LEFT_PAD, :] = jnp.zeros((B_LEFT_PAD, B_WIDTH), BF16)
        vt_s[:, 0:B_LEFT_PAD] = jnp.zeros((B_WIDTH, B_LEFT_PAD), BF16)

    x = x_ref[...]
    ms = jnp.mean(x * x, axis=-1, keepdims=True)
    xn = x * lax.rsqrt(ms + EPS)
    hk_s[...] = (xn * gkv_ref[...]).astype(BF16)
    hq_s[...] = (xn * gb_ref[...]).astype(BF16)

    scale = B_HEAD_DIM ** -0.5
    k_s[cur, :] = _dot(hk_s[...], wk_ref[...]).astype(BF16)
    vt_s[:, cur] = _dot_nt(wvt_ref[...], hk_s[...]).astype(BF16)
    qt_s[...] = (_dot_nt(wqt_ref[...], hq_s[...]) * scale).astype(BF16)
    szt_s[...] = jax.nn.silu(_dot_nt(wzt_ref[...], hq_s[...]))

    pen = jnp.where(t == 0, NEG_INF, 0.0).astype(F32)
    row_head = lax.broadcasted_iota(jnp.int32, (GROUP_W, B_PAIR), 0) // B_HEAD_DIM
    n_blk = B_WIN // LANES
    hist_blk = B_LEFT_PAD // LANES

    def scores(i, g):
        ts = slice(i * B_PAIR, (i + 1) * B_PAIR)
        ws = slice(i * B_PAIR, i * B_PAIR + B_WIN)
        gs = slice(g * GROUP_W, (g + 1) * GROUP_W)
        qt = qt_s[gs, ts]
        zero = jnp.zeros_like(qt)
        q4 = jnp.concatenate([jnp.where(row_head == a, qt, zero) for a in range(GROUP_HEADS)],
                             axis=1)
        return _dot(k_s[ws, gs], q4) + bias_s[g]

    def attend(i, g, s):
        ts = slice(i * B_PAIR, (i + 1) * B_PAIR)
        ws = slice(i * B_PAIR, i * B_PAIR + B_WIN)
        gs = slice(g * GROUP_W, (g + 1) * GROUP_W)
        blocks = [s[j * LANES:(j + 1) * LANES] for j in range(n_blk)]
        before_start = [i + j < hist_blk for j in range(n_blk)]
        m8 = None
        for j in range(n_blk):
            bm = _fold_rows(blocks[j], jnp.maximum)
            if before_start[j]:
                bm = bm + pen
            m8 = bm if m8 is None else jnp.maximum(m8, bm)
        m = jnp.max(m8, axis=0, keepdims=True)
        m_pen = m - pen
        l8 = None
        probs = []
        for j in range(n_blk):
            e = jnp.exp(blocks[j] - (m_pen if before_start[j] else m))
            le = _fold_rows(e, jnp.add)
            l8 = le if l8 is None else l8 + le
            probs.append(e.astype(BF16))
        inv_l = 1.0 / jnp.sum(l8, axis=0, keepdims=True)
        ot = _dot(vt_s[gs, ws], jnp.concatenate(probs, axis=0))
        for a in range(GROUP_HEADS):
            rs = slice(g * GROUP_W + a * B_HEAD_DIM, g * GROUP_W + (a + 1) * B_HEAD_DIM)
            cs = slice(a * B_PAIR, (a + 1) * B_PAIR)
            o = ot[a * B_HEAD_DIM:(a + 1) * B_HEAD_DIM, cs] * inv_l[:, cs]
            ogt_s[rs, ts] = (o * szt_s[rs, ts]).astype(BF16)

    bodies = [(i, g) for i in range(TILE_B // B_PAIR) for g in range(N_GROUP)]
    s_next = scores(*bodies[0])
    for n, (i, g) in enumerate(bodies):
        s = s_next
        if n + 1 < len(bodies):
            s_next = scores(*bodies[n + 1])
        attend(i, g, s)

    y = x_ref[...] + _dot(ogt_s[...].T, wout_ref[...])
    ms = jnp.mean(y * y, axis=-1, keepdims=True)
    o_ref[...] = y * lax.rsqrt(ms + EPS) * gf_ref[...]

    k_s[0:B_LEFT_PAD, :] = k_s[cur, :]
    vt_s[:, 0:B_LEFT_PAD] = vt_s[:, cur]


def _layer_b(x2d, batch, g_kv, g_b, w_k, w_vt, w_qt, w_zt, ext, w_out, g_f):
    n_tok = x2d.shape[0]
    tiles = n_tok // batch // TILE_B
    const = lambda shape: pl.BlockSpec(shape, lambda b, t: (0,) * len(shape),
                                       pipeline_mode=pl.Buffered(1))
    return pl.pallas_call(
        _layer_b_kernel,
        grid=(batch, tiles),
        in_specs=[
            pl.BlockSpec((TILE_B, D_MODEL), lambda b, t: (b * tiles + t, 0)),
            const((1, D_MODEL)),
            const((1, D_MODEL)),
            const((D_MODEL, B_WIDTH)),
            const((B_WIDTH, D_MODEL)),
            const((B_WIDTH, D_MODEL)),
            const((B_WIDTH, D_MODEL)),
            const((B_HEADS, EXT_LEN)),
            const((B_WIDTH, D_MODEL)),
            const((1, D_MODEL)),
        ],
        out_specs=pl.BlockSpec((TILE_B, D_MODEL), lambda b, t: (b * tiles + t, 0)),
        out_shape=jax.ShapeDtypeStruct((n_tok, D_MODEL), F32),
        scratch_shapes=[
            pltpu.VMEM((TILE_B, D_MODEL), BF16),
            pltpu.VMEM((TILE_B, D_MODEL), BF16),
            pltpu.VMEM((B_LEFT_PAD + TILE_B, B_WIDTH), BF16),
            pltpu.VMEM((B_WIDTH, B_LEFT_PAD + TILE_B), BF16),
            pltpu.VMEM((B_WIDTH, TILE_B), BF16),
            pltpu.VMEM((B_WIDTH, TILE_B), F32),
            pltpu.VMEM((B_WIDTH, TILE_B), BF16),
            pltpu.VMEM((N_GROUP, B_WIN, GROUP_COLS), F32),
        ],
        compiler_params=pltpu.CompilerParams(
            dimension_semantics=("arbitrary", "arbitrary"), vmem_limit_bytes=VMEM_LIMIT_BYTES),
        name="layer_b_attention",
    )(x2d, g_kv, g_b, w_k, w_vt, w_qt, w_zt, ext, w_out, g_f)


def kernel(x, a_norm_g, a_w_in, a_ln_g, a_ln_b, a_w_s, a_b_s, a_w_out, kv_norm_g, w_kv,
           b_norm_g, b_w_qz, b_rel_bias, b_w_out, final_norm_g):
    batch, seq, d = x.shape
    assert d == D_MODEL and seq % TILE_B == 0 and (batch * seq) % TILE_A == 0
    assert a_w_in.shape[0] == 1 and b_w_qz.shape[0] == 1
    assert b_rel_bias.shape[-1] == 2 * REL_CLIP + 1
    x2d = x.reshape(batch * seq, d)
    row = lambda v: v.reshape(1, -1).astype(F32)

    x1 = _layer_a(x2d, row(a_norm_g[0]), a_w_in[0].astype(BF16), row(a_ln_g[0]), row(a_ln_b[0]),
                  a_w_s[0], a_b_s[0].T, a_w_out[0].astype(BF16))
    w_qz = b_w_qz[0]
    ext = jnp.pad(b_rel_bias[0].astype(F32), ((0, 0), (0, EXT_LEN - b_rel_bias.shape[-1])),
                  mode="edge")
    out = _layer_b(x1, batch, row(kv_norm_g), row(b_norm_g[0]),
                   w_kv[:, :B_WIDTH].astype(BF16), w_kv[:, B_WIDTH:].T.astype(BF16),
                   w_qz[:, :B_WIDTH].T.astype(BF16), w_qz[:, B_WIDTH:].T.astype(BF16),
                   ext, b_w_out[0].astype(BF16), row(final_norm_g))
    return out.reshape(batch, seq, d)
```

```python
import jax
import jax.numpy as jnp
from jax import lax
from jax.experimental import pallas as pl
from jax.experimental.pallas import tpu as pltpu

D_MODEL = 1024
CHUNK = 64
A_WIDTH = 2 * D_MODEL
A_GROUP_LEN = 128
A_HEAD_CH = 128
A_HEADS = A_WIDTH // A_HEAD_CH
B_HEAD_DIM = 64
B_HEADS = D_MODEL // B_HEAD_DIM
B_WIDTH = B_HEADS * B_HEAD_DIM
B_PREV_CHUNKS = 8
B_LEFT_PAD = B_PREV_CHUNKS * CHUNK
REL_CLIP = 256
EPS = 1e-6
NEG_INF = -1e30

LANES = 128
SUBLANES = 8
MXU_WIDTH = 256
VMEM_LIMIT_BYTES = 56 * 1024 * 1024

TILE_A = 256
TILE_B = B_LEFT_PAD
COL = MXU_WIDTH
UZ_AHEAD = 3

BF16 = jnp.bfloat16
F32 = jnp.float32


def _dot(a, b):
    return jnp.dot(a, b, preferred_element_type=F32)


def _dot_nt(a, b):
    return lax.dot_general(a, b, (((1,), (1,)), ((), ())), preferred_element_type=F32)


def _lane_fold(v):
    acc = v[:, :LANES]
    for i in range(1, v.shape[1] // LANES):
        acc = acc + v[:, i * LANES:(i + 1) * LANES]
    return acc


def _layer_a_kernel(x_ref, g_ref, win_ref, lng_ref, lnb_ref, ws_ref, bst_ref, wout_ref,
                    o_ref, h_s, gv_s, vn_s, y_s):
    n_col = A_WIDTH // COL
    n_grp = TILE_A // A_GROUP_LEN

    x = x_ref[...]
    ms = jnp.mean(x * x, axis=-1, keepdims=True)
    h_s[...] = (x * lax.rsqrt(ms + EPS) * g_ref[...]).astype(BF16)

    acc = jnp.zeros((TILE_A, LANES), F32)
    for c in range(n_col):
        cs = slice(c * COL, (c + 1) * COL)
        gv = jax.nn.gelu(_dot(h_s[...], win_ref[:, A_WIDTH + c * COL:A_WIDTH + (c + 1) * COL]))
        gv_s[:, cs] = gv
        acc = acc + _lane_fold(gv)
    def uz_dots(c):
        return (_dot(h_s[...], win_ref[:, c * COL:(c + 1) * COL]),
                _dot(h_s[...], win_ref[:, 2 * A_WIDTH + c * COL:2 * A_WIDTH + (c + 1) * COL]))

    uz = {c: uz_dots(c) for c in range(UZ_AHEAD)}

    mu = jnp.sum(acc, axis=-1, keepdims=True) * (1.0 / A_WIDTH)
    acc = jnp.zeros((TILE_A, LANES), F32)
    for c in range(n_col):
        d = gv_s[:, c * COL:(c + 1) * COL] - mu
        acc = acc + _lane_fold(d * d)
    rstd = lax.rsqrt(jnp.sum(acc, axis=-1, keepdims=True) * (1.0 / A_WIDTH) + EPS)
    for c in range(n_col):
        cs = slice(c * COL, (c + 1) * COL)
        vn = (gv_s[:, cs] - mu) * rstd * lng_ref[:, cs] + lnb_ref[:, cs]
        vn_s[:, cs] = vn.astype(BF16)

    t_idx = lax.broadcasted_iota(jnp.int32, (A_GROUP_LEN, A_GROUP_LEN), 0) // CHUNK
    s_idx = lax.broadcasted_iota(jnp.int32, (A_GROUP_LEN, A_GROUP_LEN), 1) // CHUNK
    allowed = t_idx >= s_idx

    for c in range(n_col):
        if c + UZ_AHEAD < n_col:
            uz[c + UZ_AHEAD] = uz_dots(c + UZ_AHEAD)
        u_raw, z_raw = uz.pop(c)
        u = jax.nn.gelu(u_raw)
        sz = jax.nn.silu(z_raw)
        for hh in range(COL // A_HEAD_CH):
            hd = c * (COL // A_HEAD_CH) + hh
            hs = slice(hd * A_HEAD_CH, (hd + 1) * A_HEAD_CH)
            ls = slice(hh * A_HEAD_CH, (hh + 1) * A_HEAD_CH)
            wm = jnp.where(allowed, ws_ref[hd], 0.0).astype(BF16)
            bias = jnp.broadcast_to(bst_ref[:, hd:hd + 1], (A_GROUP_LEN, A_HEAD_CH))
            for g in range(n_grp):
                rs = slice(g * A_GROUP_LEN, (g + 1) * A_GROUP_LEN)
                mixed = _dot(wm, vn_s[rs, hs]) + bias
                y_s[rs, hs] = (u[rs, ls] * mixed * sz[rs, ls]).astype(BF16)

    o_ref[...] = x_ref[...] + _dot(y_s[...], wout_ref[...])


def _layer_a(x2d, g, w_in, ln_g, ln_b, w_s, b_st, w_out):
    n_tok = x2d.shape[0]
    const = lambda shape: pl.BlockSpec(shape, lambda i: (0,) * len(shape),
                                       pipeline_mode=pl.Buffered(1))
    return pl.pallas_call(
        _layer_a_kernel,
        grid=(n_tok // TILE_A,),
        in_specs=[
            pl.BlockSpec((TILE_A, D_MODEL), lambda i: (i, 0)),
            const((1, D_MODEL)),
            const((D_MODEL, 3 * A_WIDTH)),
            const((1, A_WIDTH)),
            const((1, A_WIDTH)),
            const((A_HEADS, A_GROUP_LEN, A_GROUP_LEN)),
            const((A_GROUP_LEN, A_HEADS)),
            const((A_WIDTH, D_MODEL)),
        ],
        out_specs=pl.BlockSpec((TILE_A, D_MODEL), lambda i: (i, 0)),
        out_shape=jax.ShapeDtypeStruct((n_tok, D_MODEL), F32),
        scratch_shapes=[
            pltpu.VMEM((TILE_A, D_MODEL), BF16),
            pltpu.VMEM((TILE_A, A_WIDTH), F32),
            pltpu.VMEM((TILE_A, A_WIDTH), BF16),
            pltpu.VMEM((TILE_A, A_WIDTH), BF16),
        ],
        compiler_params=pltpu.CompilerParams(
            dimension_semantics=("arbitrary",), vmem_limit_bytes=VMEM_LIMIT_BYTES),
        name="layer_a_gmlp",
    )(x2d, g, w_in, ln_g, ln_b, w_s, b_st, w_out)


B_PAIR = 2 * CHUNK
B_WIN = B_LEFT_PAD + B_PAIR
GROUP_HEADS = MXU_WIDTH // B_HEAD_DIM
GROUP_W = GROUP_HEADS * B_HEAD_DIM
N_GROUP = B_HEADS // GROUP_HEADS
GROUP_COLS = GROUP_HEADS * B_PAIR
EXT_LEN = 1024


def _fold_rows(v, op):
    parts = [v[r:r + SUBLANES] for r in range(0, v.shape[0], SUBLANES)]
    while len(parts) > 1:
        nxt = [op(parts[k], parts[k + 1]) for k in range(0, len(parts) - 1, 2)]
        if len(parts) % 2:
            nxt.append(parts[-1])
        parts = nxt
    return parts[0]


def _build_bias_table(ext_ref, bias_s):
    r_in = lax.broadcasted_iota(jnp.int32, (LANES, LANES), 0)
    c_in = lax.broadcasted_iota(jnp.int32, (LANES, LANES), 1)
    parity = c_in // CHUNK
    for jb in range(B_WIN // LANES):
        key_chunk = 2 * jb + r_in // CHUNK
        in_band = jnp.logical_and(key_chunk >= parity, key_chunk <= parity + B_PREV_CHUNKS)
        start = B_WIN - LANES * jb
        for h in range(B_HEADS):
            g, a = divmod(h, GROUP_HEADS)
            row = jnp.broadcast_to(ext_ref[h:h + 1, start:start + 2 * LANES], (LANES, 2 * LANES))
            toep = pltpu.roll(row, 0, 1, stride=1, stride_axis=0)[:, LANES:]
            bias_s[g, jb * LANES:(jb + 1) * LANES, a * LANES:(a + 1) * LANES] = jnp.where(
                in_band, toep, NEG_INF)


def _layer_b_kernel(x_ref, gkv_ref, gb_ref, wk_ref, wvt_ref, wqt_ref, wzt_ref, ext_ref, wout_ref,
                    gf_ref, o_ref, hk_s, hq_s, k_s, vt_s, qt_s, szt_s, ogt_s, bias_s):
    b = pl.program_id(0)
    t = pl.program_id(1)
    cur = slice(B_LEFT_PAD, B_LEFT_PAD + TILE_B)

    @pl.when(jnp.logical_and(b == 0, t == 0))
    def _():
        _build_bias_table(ext_ref, bias_s)

    @pl.when(t == 0)
    def _():
        k_s[0:B_LEFT_PAD, :] = jnp.zeros((B_LEFT_PAD, B_WIDTH), BF16)
        vt_s[:, 0:B_LEFT_PAD] = jnp.zeros((B_WIDTH, B_LEFT_PAD), BF16)

    x = x_ref[...]
    ms = jnp.mean(x * x, axis=-1, keepdims=True)
    xn = x * lax.rsqrt(ms + EPS)
    hk_s[...] = (xn * gkv_ref[...]).astype(BF16)
    hq_s[...] = (xn * gb_ref[...]).astype(BF16)

    scale = B_HEAD_DIM ** -0.5
    k_s[cur, :] = _dot(hk_s[...], wk_ref[...]).astype(BF16)
    vt_s[:, cur] = _dot_nt(wvt_ref[...], hk_s[...]).astype(BF16)
    qt_s[...] = (_dot_nt(wqt_ref[...], hq_s[...]) * scale).astype(BF16)
    szt_s[...] = jax.nn.silu(_dot_nt(wzt_ref[...], hq_s[...]))

    pen = jnp.where(t == 0, NEG_INF, 0.0).astype(F32)
    row_head = lax.broadcasted_iota(jnp.int32, (GROUP_W, B_PAIR), 0) // B_HEAD_DIM
    n_blk = B_WIN // LANES
    hist_blk = B_LEFT_PAD // LANES

    def scores(i, g):
        ts = slice(i * B_PAIR, (i + 1) * B_PAIR)
        ws = slice(i * B_PAIR, i * B_PAIR + B_WIN)
        gs = slice(g * GROUP_W, (g + 1) * GROUP_W)
        qt = qt_s[gs, ts]
        zero = jnp.zeros_like(qt)
        q4 = jnp.concatenate([jnp.where(row_head == a, qt, zero) for a in range(GROUP_HEADS)],
                             axis=1)
        return _dot(k_s[ws, gs], q4) + bias_s[g]

    def attend(i, g, s):
        ts = slice(i * B_PAIR, (i + 1) * B_PAIR)
        ws = slice(i * B_PAIR, i * B_PAIR + B_WIN)
        gs = slice(g * GROUP_W, (g + 1) * GROUP_W)
        blocks = [s[j * LANES:(j + 1) * LANES] for j in range(n_blk)]
        before_start = [i + j < hist_blk for j in range(n_blk)]
        m8 = None
        for j in range(n_blk):
            bm = _fold_rows(blocks[j], jnp.maximum)
            if before_start[j]:
                bm = bm + pen
            m8 = bm if m8 is None else jnp.maximum(m8, bm)
        m = jnp.max(m8, axis=0, keepdims=True)
        m_pen = m - pen
        l8 = None
        probs = []
        for j in range(n_blk):
            e = jnp.exp(blocks[j] - (m_pen if before_start[j] else m))
            le = _fold_rows(e, jnp.add)
            l8 = le if l8 is None else l8 + le
            probs.append(e.astype(BF16))
        inv_l = 1.0 / jnp.sum(l8, axis=0, keepdims=True)
        ot = _dot(vt_s[gs, ws], jnp.concatenate(probs, axis=0))
        for a in range(GROUP_HEADS):
            rs = slice(g * GROUP_W + a * B_HEAD_DIM, g * GROUP_W + (a + 1) * B_HEAD_DIM)
            cs = slice(a * B_PAIR, (a + 1) * B_PAIR)
            o = ot[a * B_HEAD_DIM:(a + 1) * B_HEAD_DIM, cs] * inv_l[:, cs]
            ogt_s[rs, ts] = (o * szt_s[rs, ts]).astype(BF16)

    bodies = [(i, g) for i in range(TILE_B // B_PAIR) for g in range(N_GROUP)]
    s_next = scores(*bodies[0])
    for n, (i, g) in enumerate(bodies):
        s = s_next
        if n + 1 < len(bodies):
            s_next = scores(*bodies[n + 1])
        attend(i, g, s)

    y = x_ref[...] + _dot(ogt_s[...].T, wout_ref[...])
    ms = jnp.mean(y * y, axis=-1, keepdims=True)
    o_ref[...] = y * lax.rsqrt(ms + EPS) * gf_ref[...]

    k_s[0:B_LEFT_PAD, :] = k_s[cur, :]
    vt_s[:, 0:B_LEFT_PAD] = vt_s[:, cur]


def _layer_b(x2d, batch, g_kv, g_b, w_k, w_vt, w_qt, w_zt, ext, w_out, g_f):
    n_tok = x2d.shape[0]
    tiles = n_tok // batch // TILE_B
    const = lambda shape: pl.BlockSpec(shape, lambda b, t: (0,) * len(shape),
                                       pipeline_mode=pl.Buffered(1))
    return pl.pallas_call(
        _layer_b_kernel,
        grid=(batch, tiles),
        in_specs=[
            pl.BlockSpec((TILE_B, D_MODEL), lambda b, t: (b * tiles + t, 0)),
            const((1, D_MODEL)),
            const((1, D_MODEL)),
            const((D_MODEL, B_WIDTH)),
            const((B_WIDTH, D_MODEL)),
            const((B_WIDTH, D_MODEL)),
            const((B_WIDTH, D_MODEL)),
            const((B_HEADS, EXT_LEN)),
            const((B_WIDTH, D_MODEL)),
            const((1, D_MODEL)),
        ],
        out_specs=pl.BlockSpec((TILE_B, D_MODEL), lambda b, t: (b * tiles + t, 0)),
        out_shape=jax.ShapeDtypeStruct((n_tok, D_MODEL), F32),
        scratch_shapes=[
            pltpu.VMEM((TILE_B, D_MODEL), BF16),
            pltpu.VMEM((TILE_B, D_MODEL), BF16),
            pltpu.VMEM((B_LEFT_PAD + TILE_B, B_WIDTH), BF16),
            pltpu.VMEM((B_WIDTH, B_LEFT_PAD + TILE_B), BF16),
            pltpu.VMEM((B_WIDTH, TILE_B), BF16),
            pltpu.VMEM((B_WIDTH, TILE_B), F32),
            pltpu.VMEM((B_WIDTH, TILE_B), BF16),
            pltpu.VMEM((N_GROUP, B_WIN, GROUP_COLS), F32),
        ],
        compiler_params=pltpu.CompilerParams(
            dimension_semantics=("arbitrary", "arbitrary"), vmem_limit_bytes=VMEM_LIMIT_BYTES),
        name="layer_b_attention",
    )(x2d, g_kv, g_b, w_k, w_vt, w_qt, w_zt, ext, w_out, g_f)


def kernel(x, a_norm_g, a_w_in, a_ln_g, a_ln_b, a_w_s, a_b_s, a_w_out, kv_norm_g, w_kv,
           b_norm_g, b_w_qz, b_rel_bias, b_w_out, final_norm_g):
    batch, seq, d = x.shape
    assert d == D_MODEL and seq % TILE_B == 0 and (batch * seq) % TILE_A == 0
    assert a_w_in.shape[0] == 1 and b_w_qz.shape[0] == 1
    assert b_rel_bias.shape[-1] == 2 * REL_CLIP + 1
    x2d = x.reshape(batch * seq, d)
    row = lambda v: v.reshape(1, -1).astype(F32)

    x1 = _layer_a(x2d, row(a_norm_g[0]), a_w_in[0].astype(BF16), row(a_ln_g[0]), row(a_ln_b[0]),
                  a_w_s[0], a_b_s[0].T, a_w_out[0].astype(BF16))
    w_qz = b_w_qz[0]
    ext = jnp.pad(b_rel_bias[0].astype(F32), ((0, 0), (0, EXT_LEN - b_rel_bias.shape[-1])),
                  mode="edge")
    out = _layer_b(x1, batch, row(kv_norm_g), row(b_norm_g[0]),
                   w_kv[:, :B_WIDTH].astype(BF16), w_kv[:, B_WIDTH:].T.astype(BF16),
                   w_qz[:, :B_WIDTH].T.astype(BF16), w_qz[:, B_WIDTH:].T.astype(BF16),
                   ext, b_w_out[0].astype(BF16), row(final_norm_g))
    return out.reshape(batch, seq, d)
```

```python
import jax
import jax.numpy as jnp
from jax import lax
from jax.experimental import pallas as pl
from jax.experimental.pallas import tpu as pltpu

D_MODEL = 1024
CHUNK = 64
A_WIDTH = 2 * D_MODEL
A_GROUP_LEN = 128
A_HEAD_CH = 128
A_HEADS = A_WIDTH // A_HEAD_CH
B_HEAD_DIM = 64
B_HEADS = D_MODEL // B_HEAD_DIM
B_WIDTH = B_HEADS * B_HEAD_DIM
B_PREV_CHUNKS = 8
B_LEFT_PAD = B_PREV_CHUNKS * CHUNK
REL_CLIP = 256
EPS = 1e-6
NEG_INF = -1e30
LOG2_E = 1.4426950408889634

LANES = 128
SUBLANES = 8
MXU_WIDTH = 256
VMEM_LIMIT_BYTES = 56 * 1024 * 1024

TILE_A = 256
TILE_B = B_LEFT_PAD
COL = MXU_WIDTH
UZ_AHEAD = 3

BF16 = jnp.bfloat16
F32 = jnp.float32


def _dot(a, b):
    return jnp.dot(a, b, preferred_element_type=F32)


def _dot_nt(a, b):
    return lax.dot_general(a, b, (((1,), (1,)), ((), ())), preferred_element_type=F32)


def _silu(v):
    half = 0.5 * v
    return half + half * jnp.tanh(half)


def _lane_fold(v):
    acc = v[:, :LANES]
    for i in range(1, v.shape[1] // LANES):
        acc = acc + v[:, i * LANES:(i + 1) * LANES]
    return acc


def _layer_a_kernel(x_ref, g_ref, win_ref, lng_ref, lnb_ref, ws_ref, bst_ref, wout_ref,
                    o_ref, h_s, gv_s, vn_s, y_s):
    n_col = A_WIDTH // COL
    n_grp = TILE_A // A_GROUP_LEN

    x = x_ref[...]
    ms = jnp.mean(x * x, axis=-1, keepdims=True)
    h_s[...] = (x * lax.rsqrt(ms + EPS) * g_ref[...]).astype(BF16)

    acc = jnp.zeros((TILE_A, LANES), F32)
    for c in range(n_col):
        cs = slice(c * COL, (c + 1) * COL)
        gv = jax.nn.gelu(_dot(h_s[...], win_ref[:, A_WIDTH + c * COL:A_WIDTH + (c + 1) * COL]))
        gv_s[:, cs] = gv
        acc = acc + _lane_fold(gv)
    def uz_dots(c):
        return (_dot(h_s[...], win_ref[:, c * COL:(c + 1) * COL]),
                _dot(h_s[...], win_ref[:, 2 * A_WIDTH + c * COL:2 * A_WIDTH + (c + 1) * COL]))

    uz = {c: uz_dots(c) for c in range(UZ_AHEAD)}

    mu = jnp.sum(acc, axis=-1, keepdims=True) * (1.0 / A_WIDTH)
    acc = jnp.zeros((TILE_A, LANES), F32)
    for c in range(n_col):
        d = gv_s[:, c * COL:(c + 1) * COL] - mu
        acc = acc + _lane_fold(d * d)
    rstd = lax.rsqrt(jnp.sum(acc, axis=-1, keepdims=True) * (1.0 / A_WIDTH) + EPS)
    for c in range(n_col):
        cs = slice(c * COL, (c + 1) * COL)
        vn = (gv_s[:, cs] - mu) * rstd * lng_ref[:, cs] + lnb_ref[:, cs]
        vn_s[:, cs] = vn.astype(BF16)

    t_idx = lax.broadcasted_iota(jnp.int32, (A_GROUP_LEN, A_GROUP_LEN), 0) // CHUNK
    s_idx = lax.broadcasted_iota(jnp.int32, (A_GROUP_LEN, A_GROUP_LEN), 1) // CHUNK
    allowed = t_idx >= s_idx

    for c in range(n_col):
        if c + UZ_AHEAD < n_col:
            uz[c + UZ_AHEAD] = uz_dots(c + UZ_AHEAD)
        u_raw, z_raw = uz.pop(c)
        u = jax.nn.gelu(u_raw)
        sz = _silu(z_raw)
        for hh in range(COL // A_HEAD_CH):
            hd = c * (COL // A_HEAD_CH) + hh
            hs = slice(hd * A_HEAD_CH, (hd + 1) * A_HEAD_CH)
            ls = slice(hh * A_HEAD_CH, (hh + 1) * A_HEAD_CH)
            wm = jnp.where(allowed, ws_ref[hd], 0.0).astype(BF16)
            bias = jnp.broadcast_to(bst_ref[:, hd:hd + 1], (A_GROUP_LEN, A_HEAD_CH))
            for g in range(n_grp):
                rs = slice(g * A_GROUP_LEN, (g + 1) * A_GROUP_LEN)
                mixed = _dot(wm, vn_s[rs, hs]) + bias
                y_s[rs, hs] = (u[rs, ls] * mixed * sz[rs, ls]).astype(BF16)

    o_ref[...] = x_ref[...] + _dot(y_s[...], wout_ref[...])


def _layer_a(x2d, g, w_in, ln_g, ln_b, w_s, b_st, w_out):
    n_tok = x2d.shape[0]
    const = lambda shape: pl.BlockSpec(shape, lambda i: (0,) * len(shape),
                                       pipeline_mode=pl.Buffered(1))
    return pl.pallas_call(
        _layer_a_kernel,
        grid=(n_tok // TILE_A,),
        in_specs=[
            pl.BlockSpec((TILE_A, D_MODEL), lambda i: (i, 0)),
            const((1, D_MODEL)),
            const((D_MODEL, 3 * A_WIDTH)),
            const((1, A_WIDTH)),
            const((1, A_WIDTH)),
            const((A_HEADS, A_GROUP_LEN, A_GROUP_LEN)),
            const((A_GROUP_LEN, A_HEADS)),
            const((A_WIDTH, D_MODEL)),
        ],
        out_specs=pl.BlockSpec((TILE_A, D_MODEL), lambda i: (i, 0)),
        out_shape=jax.ShapeDtypeStruct((n_tok, D_MODEL), F32),
        scratch_shapes=[
            pltpu.VMEM((TILE_A, D_MODEL), BF16),
            pltpu.VMEM((TILE_A, A_WIDTH), F32),
            pltpu.VMEM((TILE_A, A_WIDTH), BF16),
            pltpu.VMEM((TILE_A, A_WIDTH), BF16),
        ],
        compiler_params=pltpu.CompilerParams(
            dimension_semantics=("arbitrary",), vmem_limit_bytes=VMEM_LIMIT_BYTES),
        name="layer_a_gmlp",
    )(x2d, g, w_in, ln_g, ln_b, w_s, b_st, w_out)


B_PAIR = 2 * CHUNK
B_WIN = B_LEFT_PAD + B_PAIR
GROUP_HEADS = MXU_WIDTH // B_HEAD_DIM
GROUP_W = GROUP_HEADS * B_HEAD_DIM
N_GROUP = B_HEADS // GROUP_HEADS
GROUP_COLS = GROUP_HEADS * B_PAIR
EXT_LEN = 1024


def _fold_rows(v, op):
    parts = [v[r:r + SUBLANES] for r in range(0, v.shape[0], SUBLANES)]
    while len(parts) > 1:
        nxt = [op(parts[k], parts[k + 1]) for k in range(0, len(parts) - 1, 2)]
        if len(parts) % 2:
            nxt.append(parts[-1])
        parts = nxt
    return parts[0]


def _build_bias_table(ext_ref, bias_s):
    r_in = lax.broadcasted_iota(jnp.int32, (LANES, LANES), 0)
    c_in = lax.broadcasted_iota(jnp.int32, (LANES, LANES), 1)
    parity = c_in // CHUNK
    for jb in range(B_WIN // LANES):
        key_chunk = 2 * jb + r_in // CHUNK
        in_band = jnp.logical_and(key_chunk >= parity, key_chunk <= parity + B_PREV_CHUNKS)
        start = B_WIN - LANES * jb
        for h in range(B_HEADS):
            g, a = divmod(h, GROUP_HEADS)
            row = jnp.broadcast_to(ext_ref[h:h + 1, start:start + 2 * LANES], (LANES, 2 * LANES))
            toep = pltpu.roll(row, 0, 1, stride=1, stride_axis=0)[:, LANES:]
            bias_s[g, jb * LANES:(jb + 1) * LANES, a * LANES:(a + 1) * LANES] = jnp.where(
                in_band, toep * LOG2_E, NEG_INF)


def _layer_b_kernel(x_ref, gkv_ref, gb_ref, wk_ref, wvt_ref, wqt_ref, wzt_ref, ext_ref, wout_ref,
                    gf_ref, o_ref, hk_s, hq_s, k_s, vt_s, qt_s, szt_s, ogt_s, bias_s):
    b = pl.program_id(0)
    t = pl.program_id(1)
    cur = slice(B_LEFT_PAD, B_LEFT_PAD + TILE_B)

    @pl.when(jnp.logical_and(b == 0, t == 0))
    def _():
        _build_bias_table(ext_ref, bias_s)

    @pl.when(t == 0)
    def _():
        k_s[0:B_LEFT_PAD, :] = jnp.zeros((B_LEFT_PAD, B_WIDTH), BF16)
        vt_s[:, 0:B_LEFT_PAD] = jnp.zeros((B_WIDTH, B_LEFT_PAD), BF16)

    x = x_ref[...]
    ms = jnp.mean(x * x, axis=-1, keepdims=True)
    xn = x * lax.rsqrt(ms + EPS)
    hk_s[...] = (xn * gkv_ref[...]).astype(BF16)
    hq_s[...] = (xn * gb_ref[...]).astype(BF16)

    scale = B_HEAD_DIM ** -0.5 * LOG2_E
    k_s[cur, :] = _dot(hk_s[...], wk_ref[...]).astype(BF16)
    vt_s[:, cur] = _dot_nt(wvt_ref[...], hk_s[...]).astype(BF16)
    qt_s[...] = (_dot_nt(wqt_ref[...], hq_s[...]) * scale).astype(BF16)
    szt_s[...] = _silu(_dot_nt(wzt_ref[...], hq_s[...]))

    pen = jnp.where(t == 0, NEG_INF, 0.0).astype(F32)
    row_head = lax.broadcasted_iota(jnp.int32, (GROUP_W, B_PAIR), 0) // B_HEAD_DIM
    n_blk = B_WIN // LANES
    hist_blk = B_LEFT_PAD // LANES

    def scores(i, g):
        ts = slice(i * B_PAIR, (i + 1) * B_PAIR)
        ws = slice(i * B_PAIR, i * B_PAIR + B_WIN)
        gs = slice(g * GROUP_W, (g + 1) * GROUP_W)
        qt = qt_s[gs, ts]
        zero = jnp.zeros_like(qt)
        q4 = jnp.concatenate([jnp.where(row_head == a, qt, zero) for a in range(GROUP_HEADS)],
                             axis=1)
        return _dot(k_s[ws, gs], q4) + bias_s[g]

    def attend(i, g, s):
        ts = slice(i * B_PAIR, (i + 1) * B_PAIR)
        ws = slice(i * B_PAIR, i * B_PAIR + B_WIN)
        gs = slice(g * GROUP_W, (g + 1) * GROUP_W)
        blocks = [s[j * LANES:(j + 1) * LANES] for j in range(n_blk)]
        before_start = [i + j < hist_blk for j in range(n_blk)]
        m8 = None
        for j in range(n_blk):
            bm = _fold_rows(blocks[j], jnp.maximum)
            if before_start[j]:
                bm = bm + pen
            m8 = bm if m8 is None else jnp.maximum(m8, bm)
        m = jnp.max(m8, axis=0, keepdims=True)
        m_pen = m - pen
        l8 = None
        probs = []
        for j in range(n_blk):
            e = jnp.exp2(blocks[j] - (m_pen if before_start[j] else m))
            le = _fold_rows(e, jnp.add)
            l8 = le if l8 is None else l8 + le
            probs.append(e.astype(BF16))
        inv_l = 1.0 / jnp.sum(l8, axis=0, keepdims=True)
        ot = _dot(vt_s[gs, ws], jnp.concatenate(probs, axis=0))
        for a in range(GROUP_HEADS):
            rs = slice(g * GROUP_W + a * B_HEAD_DIM, g * GROUP_W + (a + 1) * B_HEAD_DIM)
            cs = slice(a * B_PAIR, (a + 1) * B_PAIR)
            o = ot[a * B_HEAD_DIM:(a + 1) * B_HEAD_DIM, cs] * inv_l[:, cs]
            ogt_s[rs, ts] = (o * szt_s[rs, ts]).astype(BF16)

    bodies = [(i, g) for i in range(TILE_B // B_PAIR) for g in range(N_GROUP)]
    s_next = scores(*bodies[0])
    for n, (i, g) in enumerate(bodies):
        s = s_next
        if n + 1 < len(bodies):
            s_next = scores(*bodies[n + 1])
        attend(i, g, s)

    y = x_ref[...] + _dot(ogt_s[...].T, wout_ref[...])
    ms = jnp.mean(y * y, axis=-1, keepdims=True)
    o_ref[...] = y * lax.rsqrt(ms + EPS) * gf_ref[...]

    k_s[0:B_LEFT_PAD, :] = k_s[cur, :]
    vt_s[:, 0:B_LEFT_PAD] = vt_s[:, cur]


def _layer_b(x2d, batch, g_kv, g_b, w_k, w_vt, w_qt, w_zt, ext, w_out, g_f):
    n_tok = x2d.shape[0]
    tiles = n_tok // batch // TILE_B
    const = lambda shape: pl.BlockSpec(shape, lambda b, t: (0,) * len(shape),
                                       pipeline_mode=pl.Buffered(1))
    return pl.pallas_call(
        _layer_b_kernel,
        grid=(batch, tiles),
        in_specs=[
            pl.BlockSpec((TILE_B, D_MODEL), lambda b, t: (b * tiles + t, 0)),
            const((1, D_MODEL)),
            const((1, D_MODEL)),
            const((D_MODEL, B_WIDTH)),
            const((B_WIDTH, D_MODEL)),
            const((B_WIDTH, D_MODEL)),
            const((B_WIDTH, D_MODEL)),
            const((B_HEADS, EXT_LEN)),
            const((B_WIDTH, D_MODEL)),
            const((1, D_MODEL)),
        ],
        out_specs=pl.BlockSpec((TILE_B, D_MODEL), lambda b, t: (b * tiles + t, 0)),
        out_shape=jax.ShapeDtypeStruct((n_tok, D_MODEL), F32),
        scratch_shapes=[
            pltpu.VMEM((TILE_B, D_MODEL), BF16),
            pltpu.VMEM((TILE_B, D_MODEL), BF16),
            pltpu.VMEM((B_LEFT_PAD + TILE_B, B_WIDTH), BF16),
            pltpu.VMEM((B_WIDTH, B_LEFT_PAD + TILE_B), BF16),
            pltpu.VMEM((B_WIDTH, TILE_B), BF16),
            pltpu.VMEM((B_WIDTH, TILE_B), F32),
            pltpu.VMEM((B_WIDTH, TILE_B), BF16),
            pltpu.VMEM((N_GROUP, B_WIN, GROUP_COLS), F32),
        ],
        compiler_params=pltpu.CompilerParams(
            dimension_semantics=("arbitrary", "arbitrary"), vmem_limit_bytes=VMEM_LIMIT_BYTES),
        name="layer_b_attention",
    )(x2d, g_kv, g_b, w_k, w_vt, w_qt, w_zt, ext, w_out, g_f)


def kernel(x, a_norm_g, a_w_in, a_ln_g, a_ln_b, a_w_s, a_b_s, a_w_out, kv_norm_g, w_kv,
           b_norm_g, b_w_qz, b_rel_bias, b_w_out, final_norm_g):
    batch, seq, d = x.shape
    assert d == D_MODEL and seq % TILE_B == 0 and (batch * seq) % TILE_A == 0
    assert a_w_in.shape[0] == 1 and b_w_qz.shape[0] == 1
    assert b_rel_bias.shape[-1] == 2 * REL_CLIP + 1
    x2d = x.reshape(batch * seq, d)
    row = lambda v: v.reshape(1, -1).astype(F32)

    x1 = _layer_a(x2d, row(a_norm_g[0]), a_w_in[0].astype(BF16), row(a_ln_g[0]), row(a_ln_b[0]),
                  a_w_s[0], a_b_s[0].T, a_w_out[0].astype(BF16))
    w_qz = b_w_qz[0]
    ext = jnp.pad(b_rel_bias[0].astype(F32), ((0, 0), (0, EXT_LEN - b_rel_bias.shape[-1])),
                  mode="edge")
    out = _layer_b(x1, batch, row(kv_norm_g), row(b_norm_g[0]),
                   w_kv[:, :B_WIDTH].astype(BF16), w_kv[:, B_WIDTH:].T.astype(BF16),
                   w_qz[:, :B_WIDTH].T.astype(BF16), w_qz[:, B_WIDTH:].T.astype(BF16),
                   ext, b_w_out[0].astype(BF16), row(final_norm_g))
    return out.reshape(batch, seq, d)
```

```python
import jax
import jax.numpy as jnp
from jax import lax
from jax.experimental import pallas as pl
from jax.experimental.pallas import tpu as pltpu

D_MODEL = 1024
CHUNK = 64
A_WIDTH = 2 * D_MODEL
A_GROUP_LEN = 128
A_HEAD_CH = 128
A_HEADS = A_WIDTH // A_HEAD_CH
B_HEAD_DIM = 64
B_HEADS = D_MODEL // B_HEAD_DIM
B_WIDTH = B_HEADS * B_HEAD_DIM
B_PREV_CHUNKS = 8
B_LEFT_PAD = B_PREV_CHUNKS * CHUNK
REL_CLIP = 256
EPS = 1e-6
NEG_INF = -1e30
LOG2_E = 1.4426950408889634

LANES = 128
SUBLANES = 8
MXU_WIDTH = 256
VMEM_LIMIT_BYTES = 56 * 1024 * 1024

TILE_A = 256
TILE_B = B_LEFT_PAD
COL = MXU_WIDTH
UZ_AHEAD = 3

BF16 = jnp.bfloat16
F32 = jnp.float32


def _dot(a, b):
    return jnp.dot(a, b, preferred_element_type=F32)


def _dot_nt(a, b):
    return lax.dot_general(a, b, (((1,), (1,)), ((), ())), preferred_element_type=F32)


def _silu(v):
    half = 0.5 * v
    return half + half * jnp.tanh(half)


def _lane_fold(v):
    acc = v[:, :LANES]
    for i in range(1, v.shape[1] // LANES):
        acc = acc + v[:, i * LANES:(i + 1) * LANES]
    return acc


def _layer_a_kernel(x_ref, g_ref, win_ref, lng_ref, lnb_ref, ws_ref, bst_ref, wout_ref,
                    o_ref, h_s, gv_s, vn_s, y_s):
    n_col = A_WIDTH // COL
    n_grp = TILE_A // A_GROUP_LEN

    x = x_ref[...]
    ms = jnp.mean(x * x, axis=-1, keepdims=True)
    h_s[...] = (x * lax.rsqrt(ms + EPS) * g_ref[...]).astype(BF16)

    acc = jnp.zeros((TILE_A, LANES), F32)
    for c in range(n_col):
        cs = slice(c * COL, (c + 1) * COL)
        gv = jax.nn.gelu(_dot(h_s[...], win_ref[:, A_WIDTH + c * COL:A_WIDTH + (c + 1) * COL]))
        gv_s[:, cs] = gv
        acc = acc + _lane_fold(gv)
    def uz_dots(c):
        return (_dot(h_s[...], win_ref[:, c * COL:(c + 1) * COL]),
                _dot(h_s[...], win_ref[:, 2 * A_WIDTH + c * COL:2 * A_WIDTH + (c + 1) * COL]))

    uz = {c: uz_dots(c) for c in range(UZ_AHEAD)}

    mu = jnp.sum(acc, axis=-1, keepdims=True) * (1.0 / A_WIDTH)
    acc = jnp.zeros((TILE_A, LANES), F32)
    for c in range(n_col):
        d = gv_s[:, c * COL:(c + 1) * COL] - mu
        acc = acc + _lane_fold(d * d)
    rstd = lax.rsqrt(jnp.sum(acc, axis=-1, keepdims=True) * (1.0 / A_WIDTH) + EPS)
    for c in range(n_col):
        cs = slice(c * COL, (c + 1) * COL)
        vn = (gv_s[:, cs] - mu) * rstd * lng_ref[:, cs] + lnb_ref[:, cs]
        vn_s[:, cs] = vn.astype(BF16)

    t_idx = lax.broadcasted_iota(jnp.int32, (A_GROUP_LEN, A_GROUP_LEN), 0) // CHUNK
    s_idx = lax.broadcasted_iota(jnp.int32, (A_GROUP_LEN, A_GROUP_LEN), 1) // CHUNK
    allowed = t_idx >= s_idx

    for c in range(n_col):
        if c + UZ_AHEAD < n_col:
            uz[c + UZ_AHEAD] = uz_dots(c + UZ_AHEAD)
        u_raw, z_raw = uz.pop(c)
        u = jax.nn.gelu(u_raw)
        sz = _silu(z_raw)
        for hh in range(COL // A_HEAD_CH):
            hd = c * (COL // A_HEAD_CH) + hh
            hs = slice(hd * A_HEAD_CH, (hd + 1) * A_HEAD_CH)
            ls = slice(hh * A_HEAD_CH, (hh + 1) * A_HEAD_CH)
            wm = jnp.where(allowed, ws_ref[hd], 0.0).astype(BF16)
            bias = jnp.broadcast_to(bst_ref[:, hd:hd + 1], (A_GROUP_LEN, A_HEAD_CH))
            for g in range(n_grp):
                rs = slice(g * A_GROUP_LEN, (g + 1) * A_GROUP_LEN)
                mixed = _dot(wm, vn_s[rs, hs]) + bias
                y_s[rs, hs] = (u[rs, ls] * mixed * sz[rs, ls]).astype(BF16)

    o_ref[...] = x_ref[...] + _dot(y_s[...], wout_ref[...])


def _layer_a(x2d, g, w_in, ln_g, ln_b, w_s, b_st, w_out):
    n_tok = x2d.shape[0]
    const = lambda shape: pl.BlockSpec(shape, lambda i: (0,) * len(shape),
                                       pipeline_mode=pl.Buffered(1))
    return pl.pallas_call(
        _layer_a_kernel,
        grid=(n_tok // TILE_A,),
        in_specs=[
            pl.BlockSpec((TILE_A, D_MODEL), lambda i: (i, 0)),
            const((1, D_MODEL)),
            const((D_MODEL, 3 * A_WIDTH)),
            const((1, A_WIDTH)),
            const((1, A_WIDTH)),
            const((A_HEADS, A_GROUP_LEN, A_GROUP_LEN)),
            const((A_GROUP_LEN, A_HEADS)),
            const((A_WIDTH, D_MODEL)),
        ],
        out_specs=pl.BlockSpec((TILE_A, D_MODEL), lambda i: (i, 0)),
        out_shape=jax.ShapeDtypeStruct((n_tok, D_MODEL), F32),
        scratch_shapes=[
            pltpu.VMEM((TILE_A, D_MODEL), BF16),
            pltpu.VMEM((TILE_A, A_WIDTH), F32),
            pltpu.VMEM((TILE_A, A_WIDTH), BF16),
            pltpu.VMEM((TILE_A, A_WIDTH), BF16),
        ],
        compiler_params=pltpu.CompilerParams(
            dimension_semantics=("arbitrary",), vmem_limit_bytes=VMEM_LIMIT_BYTES),
        name="layer_a_gmlp",
    )(x2d, g, w_in, ln_g, ln_b, w_s, b_st, w_out)


B_PAIR = 2 * CHUNK
B_WIN = B_LEFT_PAD + B_PAIR
GROUP_HEADS = MXU_WIDTH // B_HEAD_DIM
GROUP_W = GROUP_HEADS * B_HEAD_DIM
N_GROUP = B_HEADS // GROUP_HEADS
GROUP_COLS = GROUP_HEADS * B_PAIR
EXT_LEN = 1024


def _fold_rows(v, op):
    parts = [v[r:r + SUBLANES] for r in range(0, v.shape[0], SUBLANES)]
    while len(parts) > 1:
        nxt = [op(parts[k], parts[k + 1]) for k in range(0, len(parts) - 1, 2)]
        if len(parts) % 2:
            nxt.append(parts[-1])
        parts = nxt
    return parts[0]


def _build_bias_table(ext_ref, bias_s):
    r_in = lax.broadcasted_iota(jnp.int32, (LANES, LANES), 0)
    c_in = lax.broadcasted_iota(jnp.int32, (LANES, LANES), 1)
    parity = c_in // CHUNK
    for jb in range(B_WIN // LANES):
        key_chunk = 2 * jb + r_in // CHUNK
        in_band = jnp.logical_and(key_chunk >= parity, key_chunk <= parity + B_PREV_CHUNKS)
        start = B_WIN - LANES * jb
        for h in range(B_HEADS):
            g, a = divmod(h, GROUP_HEADS)
            row = jnp.broadcast_to(ext_ref[h:h + 1, start:start + 2 * LANES], (LANES, 2 * LANES))
            toep = pltpu.roll(row, 0, 1, stride=1, stride_axis=0)[:, LANES:]
            bias_s[g, jb * LANES:(jb + 1) * LANES, a * LANES:(a + 1) * LANES] = jnp.where(
                in_band, toep * LOG2_E, NEG_INF)


def _layer_b_kernel(x_ref, gkv_ref, gb_ref, wk_ref, wvt_ref, wqt_ref, wzt_ref, ext_ref, wout_ref,
                    gf_ref, o_ref, hk_s, hq_s, k_s, vt_s, qt_s, szt_s, ogt_s, bias_s):
    b = pl.program_id(0)
    t = pl.program_id(1)
    cur = slice(B_LEFT_PAD, B_LEFT_PAD + TILE_B)

    @pl.when(jnp.logical_and(b == 0, t == 0))
    def _():
        _build_bias_table(ext_ref, bias_s)

    @pl.when(t == 0)
    def _():
        k_s[0:B_LEFT_PAD, :] = jnp.zeros((B_LEFT_PAD, B_WIDTH), BF16)
        vt_s[:, 0:B_LEFT_PAD] = jnp.zeros((B_WIDTH, B_LEFT_PAD), BF16)

    x = x_ref[...]
    ms = jnp.mean(x * x, axis=-1, keepdims=True)
    xn = x * lax.rsqrt(ms + EPS)
    hk_s[...] = (xn * gkv_ref[...]).astype(BF16)
    hq_s[...] = (xn * gb_ref[...]).astype(BF16)

    scale = B_HEAD_DIM ** -0.5 * LOG2_E

    def project(g, part):
        gs = slice(g * GROUP_W, (g + 1) * GROUP_W)
        if part == 0:
            k_s[cur, gs] = _dot(hk_s[...], wk_ref[:, gs]).astype(BF16)
        elif part == 1:
            qt_s[gs, :] = (_dot_nt(wqt_ref[gs, :], hq_s[...]) * scale).astype(BF16)
        elif part == 2:
            vt_s[gs, cur] = _dot_nt(wvt_ref[gs, :], hk_s[...]).astype(BF16)
        else:
            szt_s[gs, :] = _silu(_dot_nt(wzt_ref[gs, :], hq_s[...]))

    pen = jnp.where(t == 0, NEG_INF, 0.0).astype(F32)
    row_head = lax.broadcasted_iota(jnp.int32, (GROUP_W, B_PAIR), 0) // B_HEAD_DIM
    n_blk = B_WIN // LANES
    hist_blk = B_LEFT_PAD // LANES

    def scores(i, g):
        ts = slice(i * B_PAIR, (i + 1) * B_PAIR)
        ws = slice(i * B_PAIR, i * B_PAIR + B_WIN)
        gs = slice(g * GROUP_W, (g + 1) * GROUP_W)
        qt = qt_s[gs, ts]
        zero = jnp.zeros_like(qt)
        q4 = jnp.concatenate([jnp.where(row_head == a, qt, zero) for a in range(GROUP_HEADS)],
                             axis=1)
        return _dot(k_s[ws, gs], q4) + bias_s[g]

    def attend(i, g, s):
        ts = slice(i * B_PAIR, (i + 1) * B_PAIR)
        ws = slice(i * B_PAIR, i * B_PAIR + B_WIN)
        gs = slice(g * GROUP_W, (g + 1) * GROUP_W)
        blocks = [s[j * LANES:(j + 1) * LANES] for j in range(n_blk)]
        before_start = [i + j < hist_blk for j in range(n_blk)]
        m8 = None
        for j in range(n_blk):
            bm = _fold_rows(blocks[j], jnp.maximum)
            if before_start[j]:
                bm = bm + pen
            m8 = bm if m8 is None else jnp.maximum(m8, bm)
        m = jnp.max(m8, axis=0, keepdims=True)
        m_pen = m - pen
        l8 = None
        probs = []
        for j in range(n_blk):
            e = jnp.exp2(blocks[j] - (m_pen if before_start[j] else m))
            le = _fold_rows(e, jnp.add)
            l8 = le if l8 is None else l8 + le
            probs.append(e.astype(BF16))
        inv_l = 1.0 / jnp.sum(l8, axis=0, keepdims=True)
        ot = _dot(vt_s[gs, ws], jnp.concatenate(probs, axis=0))
        for a in range(GROUP_HEADS):
            rs = slice(g * GROUP_W + a * B_HEAD_DIM, g * GROUP_W + (a + 1) * B_HEAD_DIM)
            cs = slice(a * B_PAIR, (a + 1) * B_PAIR)
            o = ot[a * B_HEAD_DIM:(a + 1) * B_HEAD_DIM, cs] * inv_l[:, cs]
            ogt_s[rs, ts] = (o * szt_s[rs, ts]).astype(BF16)

    n_pair = TILE_B // B_PAIR
    assert n_pair == 4
    bodies = [(i, g) for g in range(N_GROUP) for i in range(n_pair)]
    for part in range(4):
        project(0, part)
    s_next = scores(*bodies[0])
    for n, (i, g) in enumerate(bodies):
        s = s_next
        if n + 1 < len(bodies):
            s_next = scores(*bodies[n + 1])
        if g + 1 < N_GROUP:
            project(g + 1, i)
        attend(i, g, s)

    y = x_ref[...] + _dot(ogt_s[...].T, wout_ref[...])
    ms = jnp.mean(y * y, axis=-1, keepdims=True)
    o_ref[...] = y * lax.rsqrt(ms + EPS) * gf_ref[...]

    k_s[0:B_LEFT_PAD, :] = k_s[cur, :]
    vt_s[:, 0:B_LEFT_PAD] = vt_s[:, cur]


def _layer_b(x2d, batch, g_kv, g_b, w_k, w_vt, w_qt, w_zt, ext, w_out, g_f):
    n_tok = x2d.shape[0]
    tiles = n_tok // batch // TILE_B
    const = lambda shape: pl.BlockSpec(shape, lambda b, t: (0,) * len(shape),
                                       pipeline_mode=pl.Buffered(1))
    return pl.pallas_call(
        _layer_b_kernel,
        grid=(batch, tiles),
        in_specs=[
            pl.BlockSpec((TILE_B, D_MODEL), lambda b, t: (b * tiles + t, 0)),
            const((1, D_MODEL)),
            const((1, D_MODEL)),
            const((D_MODEL, B_WIDTH)),
            const((B_WIDTH, D_MODEL)),
            const((B_WIDTH, D_MODEL)),
            const((B_WIDTH, D_MODEL)),
            const((B_HEADS, EXT_LEN)),
            const((B_WIDTH, D_MODEL)),
            const((1, D_MODEL)),
        ],
        out_specs=pl.BlockSpec((TILE_B, D_MODEL), lambda b, t: (b * tiles + t, 0)),
        out_shape=jax.ShapeDtypeStruct((n_tok, D_MODEL), F32),
        scratch_shapes=[
            pltpu.VMEM((TILE_B, D_MODEL), BF16),
            pltpu.VMEM((TILE_B, D_MODEL), BF16),
            pltpu.VMEM((B_LEFT_PAD + TILE_B, B_WIDTH), BF16),
            pltpu.VMEM((B_WIDTH, B_LEFT_PAD + TILE_B), BF16),
            pltpu.VMEM((B_WIDTH, TILE_B), BF16),
            pltpu.VMEM((B_WIDTH, TILE_B), F32),
            pltpu.VMEM((B_WIDTH, TILE_B), BF16),
            pltpu.VMEM((N_GROUP, B_WIN, GROUP_COLS), F32),
        ],
        compiler_params=pltpu.CompilerParams(
            dimension_semantics=("arbitrary", "arbitrary"), vmem_limit_bytes=VMEM_LIMIT_BYTES),
        name="layer_b_attention",
    )(x2d, g_kv, g_b, w_k, w_vt, w_qt, w_zt, ext, w_out, g_f)


def kernel(x, a_norm_g, a_w_in, a_ln_g, a_ln_b, a_w_s, a_b_s, a_w_out, kv_norm_g, w_kv,
           b_norm_g, b_w_qz, b_rel_bias, b_w_out, final_norm_g):
    batch, seq, d = x.shape
    assert d == D_MODEL and seq % TILE_B == 0 and (batch * seq) % TILE_A == 0
    assert a_w_in.shape[0] == 1 and b_w_qz.shape[0] == 1
    assert b_rel_bias.shape[-1] == 2 * REL_CLIP + 1
    x2d = x.reshape(batch * seq, d)
    row = lambda v: v.reshape(1, -1).astype(F32)

    x1 = _layer_a(x2d, row(a_norm_g[0]), a_w_in[0].astype(BF16), row(a_ln_g[0]), row(a_ln_b[0]),
                  a_w_s[0], a_b_s[0].T, a_w_out[0].astype(BF16))
    w_qz = b_w_qz[0]
    ext = jnp.pad(b_rel_bias[0].astype(F32), ((0, 0), (0, EXT_LEN - b_rel_bias.shape[-1])),
                  mode="edge")
    out = _layer_b(x1, batch, row(kv_norm_g), row(b_norm_g[0]),
                   w_kv[:, :B_WIDTH].astype(BF16), w_kv[:, B_WIDTH:].T.astype(BF16),
                   w_qz[:, :B_WIDTH].T.astype(BF16), w_qz[:, B_WIDTH:].T.astype(BF16),
                   ext, b_w_out[0].astype(BF16), row(final_norm_g))
    return out.reshape(batch, seq, d)
```

```python
import jax
import jax.numpy as jnp
from jax import lax
from jax.experimental import pallas as pl
from jax.experimental.pallas import tpu as pltpu

D_MODEL = 1024
CHUNK = 64
A_WIDTH = 2 * D_MODEL
A_GROUP_LEN = 128
A_HEAD_CH = 128
A_HEADS = A_WIDTH // A_HEAD_CH
B_HEAD_DIM = 64
B_HEADS = D_MODEL // B_HEAD_DIM
B_WIDTH = B_HEADS * B_HEAD_DIM
B_PREV_CHUNKS = 8
B_LEFT_PAD = B_PREV_CHUNKS * CHUNK
REL_CLIP = 256
EPS = 1e-6
NEG_INF = -1e30
LOG2_E = 1.4426950408889634

LANES = 128
SUBLANES = 8
MXU_WIDTH = 256
VMEM_LIMIT_BYTES = 56 * 1024 * 1024

TILE_A = 256
TILE_B = B_LEFT_PAD
COL = MXU_WIDTH
UZ_AHEAD = 3

BF16 = jnp.bfloat16
F32 = jnp.float32


def _dot(a, b):
    return jnp.dot(a, b, preferred_element_type=F32)


def _dot_nt(a, b):
    return lax.dot_general(a, b, (((1,), (1,)), ((), ())), preferred_element_type=F32)


def _silu(v):
    half = 0.5 * v
    return half + half * jnp.tanh(half)


def _lane_fold(v):
    acc = v[:, :LANES]
    for i in range(1, v.shape[1] // LANES):
        acc = acc + v[:, i * LANES:(i + 1) * LANES]
    return acc


def _layer_a_kernel(x_ref, g_ref, win_ref, lng_ref, lnb_ref, ws_ref, bst_ref, wout_ref,
                    o_ref, h_s, gv_s, vn_s, y_s):
    n_col = A_WIDTH // COL
    n_grp = TILE_A // A_GROUP_LEN

    x = x_ref[...]
    ms = jnp.mean(x * x, axis=-1, keepdims=True)
    h_s[...] = (x * lax.rsqrt(ms + EPS) * g_ref[...]).astype(BF16)

    acc = jnp.zeros((TILE_A, LANES), F32)
    for c in range(n_col):
        cs = slice(c * COL, (c + 1) * COL)
        gv = jax.nn.gelu(_dot(h_s[...], win_ref[:, A_WIDTH + c * COL:A_WIDTH + (c + 1) * COL]))
        gv_s[:, cs] = gv
        acc = acc + _lane_fold(gv)
    def uz_dots(c):
        return (_dot(h_s[...], win_ref[:, c * COL:(c + 1) * COL]),
                _dot(h_s[...], win_ref[:, 2 * A_WIDTH + c * COL:2 * A_WIDTH + (c + 1) * COL]))

    uz = {c: uz_dots(c) for c in range(UZ_AHEAD)}

    mu = jnp.sum(acc, axis=-1, keepdims=True) * (1.0 / A_WIDTH)
    acc = jnp.zeros((TILE_A, LANES), F32)
    for c in range(n_col):
        d = gv_s[:, c * COL:(c + 1) * COL] - mu
        acc = acc + _lane_fold(d * d)
    rstd = lax.rsqrt(jnp.sum(acc, axis=-1, keepdims=True) * (1.0 / A_WIDTH) + EPS)
    for c in range(n_col):
        cs = slice(c * COL, (c + 1) * COL)
        vn = (gv_s[:, cs] - mu) * rstd * lng_ref[:, cs] + lnb_ref[:, cs]
        vn_s[:, cs] = vn.astype(BF16)

    t_idx = lax.broadcasted_iota(jnp.int32, (A_GROUP_LEN, A_GROUP_LEN), 0) // CHUNK
    s_idx = lax.broadcasted_iota(jnp.int32, (A_GROUP_LEN, A_GROUP_LEN), 1) // CHUNK
    allowed = t_idx >= s_idx

    for c in range(n_col):
        if c + UZ_AHEAD < n_col:
            uz[c + UZ_AHEAD] = uz_dots(c + UZ_AHEAD)
        u_raw, z_raw = uz.pop(c)
        u = jax.nn.gelu(u_raw)
        sz = _silu(z_raw)
        for hh in range(COL // A_HEAD_CH):
            hd = c * (COL // A_HEAD_CH) + hh
            hs = slice(hd * A_HEAD_CH, (hd + 1) * A_HEAD_CH)
            ls = slice(hh * A_HEAD_CH, (hh + 1) * A_HEAD_CH)
            wm = jnp.where(allowed, ws_ref[hd], 0.0).astype(BF16)
            bias = jnp.broadcast_to(bst_ref[:, hd:hd + 1], (A_GROUP_LEN, A_HEAD_CH))
            for g in range(n_grp):
                rs = slice(g * A_GROUP_LEN, (g + 1) * A_GROUP_LEN)
                mixed = _dot(wm, vn_s[rs, hs]) + bias
                y_s[rs, hs] = (u[rs, ls] * mixed * sz[rs, ls]).astype(BF16)

    o_ref[...] = x_ref[...] + _dot(y_s[...], wout_ref[...])


def _layer_a(x2d, g, w_in, ln_g, ln_b, w_s, b_st, w_out):
    n_tok = x2d.shape[0]
    const = lambda shape: pl.BlockSpec(shape, lambda i: (0,) * len(shape),
                                       pipeline_mode=pl.Buffered(1))
    return pl.pallas_call(
        _layer_a_kernel,
        grid=(n_tok // TILE_A,),
        in_specs=[
            pl.BlockSpec((TILE_A, D_MODEL), lambda i: (i, 0)),
            const((1, D_MODEL)),
            const((D_MODEL, 3 * A_WIDTH)),
            const((1, A_WIDTH)),
            const((1, A_WIDTH)),
            const((A_HEADS, A_GROUP_LEN, A_GROUP_LEN)),
            const((A_GROUP_LEN, A_HEADS)),
            const((A_WIDTH, D_MODEL)),
        ],
        out_specs=pl.BlockSpec((TILE_A, D_MODEL), lambda i: (i, 0)),
        out_shape=jax.ShapeDtypeStruct((n_tok, D_MODEL), F32),
        scratch_shapes=[
            pltpu.VMEM((TILE_A, D_MODEL), BF16),
            pltpu.VMEM((TILE_A, A_WIDTH), F32),
            pltpu.VMEM((TILE_A, A_WIDTH), BF16),
            pltpu.VMEM((TILE_A, A_WIDTH), BF16),
        ],
        compiler_params=pltpu.CompilerParams(
            dimension_semantics=("arbitrary",), vmem_limit_bytes=VMEM_LIMIT_BYTES),
        name="layer_a_gmlp",
    )(x2d, g, w_in, ln_g, ln_b, w_s, b_st, w_out)


B_PAIR = 2 * CHUNK
B_WIN = B_LEFT_PAD + B_PAIR
GROUP_HEADS = MXU_WIDTH // B_HEAD_DIM
GROUP_W = GROUP_HEADS * B_HEAD_DIM
N_GROUP = B_HEADS // GROUP_HEADS
GROUP_COLS = GROUP_HEADS * B_PAIR
EXT_LEN = 1024


def _fold_rows(v, op):
    parts = [v[r:r + SUBLANES] for r in range(0, v.shape[0], SUBLANES)]
    while len(parts) > 1:
        nxt = [op(parts[k], parts[k + 1]) for k in range(0, len(parts) - 1, 2)]
        if len(parts) % 2:
            nxt.append(parts[-1])
        parts = nxt
    return parts[0]


def _build_bias_table(ext_ref, bias_s):
    r_in = lax.broadcasted_iota(jnp.int32, (LANES, LANES), 0)
    c_in = lax.broadcasted_iota(jnp.int32, (LANES, LANES), 1)
    parity = c_in // CHUNK
    for jb in range(B_WIN // LANES):
        key_chunk = 2 * jb + r_in // CHUNK
        in_band = jnp.logical_and(key_chunk >= parity, key_chunk <= parity + B_PREV_CHUNKS)
        start = B_WIN - LANES * jb
        for h in range(B_HEADS):
            g, a = divmod(h, GROUP_HEADS)
            row = jnp.broadcast_to(ext_ref[h:h + 1, start:start + 2 * LANES], (LANES, 2 * LANES))
            toep = pltpu.roll(row, 0, 1, stride=1, stride_axis=0)[:, LANES:]
            bias_s[g, jb * LANES:(jb + 1) * LANES, a * LANES:(a + 1) * LANES] = jnp.where(
                in_band, toep * LOG2_E, NEG_INF)


def _layer_b_kernel(x_ref, gkv_ref, gb_ref, wk_ref, wvt_ref, wqt_ref, wzt_ref, ext_ref, wout_ref,
                    gf_ref, o_ref, hk_s, hq_s, k_s, vt_s, qt_s, szt_s, ogt_s, bias_s):
    b = pl.program_id(0)
    t = pl.program_id(1)
    cur = slice(B_LEFT_PAD, B_LEFT_PAD + TILE_B)

    @pl.when(jnp.logical_and(b == 0, t == 0))
    def _():
        _build_bias_table(ext_ref, bias_s)

    @pl.when(t == 0)
    def _():
        k_s[0:B_LEFT_PAD, :] = jnp.zeros((B_LEFT_PAD, B_WIDTH), BF16)
        vt_s[:, 0:B_LEFT_PAD] = jnp.zeros((B_WIDTH, B_LEFT_PAD), BF16)

    x = x_ref[...]
    ms = jnp.mean(x * x, axis=-1, keepdims=True)
    xn = x * lax.rsqrt(ms + EPS)
    hk_s[...] = (xn * gkv_ref[...]).astype(BF16)
    hq_s[...] = (xn * gb_ref[...]).astype(BF16)

    scale = B_HEAD_DIM ** -0.5 * LOG2_E
    k_s[cur, :] = _dot(hk_s[...], wk_ref[...]).astype(BF16)
    vt_s[:, cur] = _dot_nt(wvt_ref[...], hk_s[...]).astype(BF16)
    qt_s[...] = (_dot_nt(wqt_ref[...], hq_s[...]) * scale).astype(BF16)
    szt_s[...] = _silu(_dot_nt(wzt_ref[...], hq_s[...]))

    pen = jnp.where(t == 0, NEG_INF, 0.0).astype(F32)
    row_head = lax.broadcasted_iota(jnp.int32, (GROUP_W, B_PAIR), 0) // B_HEAD_DIM
    n_blk = B_WIN // LANES
    hist_blk = B_LEFT_PAD // LANES

    def scores(i, g):
        ts = slice(i * B_PAIR, (i + 1) * B_PAIR)
        ws = slice(i * B_PAIR, i * B_PAIR + B_WIN)
        gs = slice(g * GROUP_W, (g + 1) * GROUP_W)
        qt = qt_s[gs, ts]
        zero = jnp.zeros_like(qt)
        q4 = jnp.concatenate([jnp.where(row_head == a, qt, zero) for a in range(GROUP_HEADS)],
                             axis=1)
        return _dot(k_s[ws, gs], q4) + bias_s[g]

    def attend(i, g, s):
        ts = slice(i * B_PAIR, (i + 1) * B_PAIR)
        ws = slice(i * B_PAIR, i * B_PAIR + B_WIN)
        gs = slice(g * GROUP_W, (g + 1) * GROUP_W)
        blocks = [s[j * LANES:(j + 1) * LANES] for j in range(n_blk)]
        before_start = [i + j < hist_blk for j in range(n_blk)]
        m8 = None
        for j in range(n_blk):
            bm = _fold_rows(blocks[j], jnp.maximum)
            if before_start[j]:
                bm = bm + pen
            m8 = bm if m8 is None else jnp.maximum(m8, bm)
        m = jnp.max(m8, axis=0, keepdims=True)
        m_pen = m - pen
        l8 = None
        probs = []
        for j in range(n_blk):
            e = jnp.exp2(blocks[j] - (m_pen if before_start[j] else m))
            le = _fold_rows(e, jnp.add)
            l8 = le if l8 is None else l8 + le
            probs.append(e.astype(BF16))
        inv_l = 1.0 / jnp.sum(l8, axis=0, keepdims=True)
        p = jnp.concatenate(probs, axis=0)
        for a in range(GROUP_HEADS):
            rs = slice(g * GROUP_W + a * B_HEAD_DIM, g * GROUP_W + (a + 1) * B_HEAD_DIM)
            cs = slice(a * B_PAIR, (a + 1) * B_PAIR)
            o = _dot(vt_s[rs, ws], p[:, cs]) * inv_l[:, cs]
            ogt_s[rs, ts] = (o * szt_s[rs, ts]).astype(BF16)

    bodies = [(i, g) for i in range(TILE_B // B_PAIR) for g in range(N_GROUP)]
    s_next = scores(*bodies[0])
    for n, (i, g) in enumerate(bodies):
        s = s_next
        if n + 1 < len(bodies):
            s_next = scores(*bodies[n + 1])
        attend(i, g, s)

    y = x_ref[...] + _dot(ogt_s[...].T, wout_ref[...])
    ms = jnp.mean(y * y, axis=-1, keepdims=True)
    o_ref[...] = y * lax.rsqrt(ms + EPS) * gf_ref[...]

    k_s[0:B_LEFT_PAD, :] = k_s[cur, :]
    vt_s[:, 0:B_LEFT_PAD] = vt_s[:, cur]


def _layer_b(x2d, batch, g_kv, g_b, w_k, w_vt, w_qt, w_zt, ext, w_out, g_f):
    n_tok = x2d.shape[0]
    tiles = n_tok // batch // TILE_B
    const = lambda shape: pl.BlockSpec(shape, lambda b, t: (0,) * len(shape),
                                       pipeline_mode=pl.Buffered(1))
    return pl.pallas_call(
        _layer_b_kernel,
        grid=(batch, tiles),
        in_specs=[
            pl.BlockSpec((TILE_B, D_MODEL), lambda b, t: (b * tiles + t, 0)),
            const((1, D_MODEL)),
            const((1, D_MODEL)),
            const((D_MODEL, B_WIDTH)),
            const((B_WIDTH, D_MODEL)),
            const((B_WIDTH, D_MODEL)),
            const((B_WIDTH, D_MODEL)),
            const((B_HEADS, EXT_LEN)),
            const((B_WIDTH, D_MODEL)),
            const((1, D_MODEL)),
        ],
        out_specs=pl.BlockSpec((TILE_B, D_MODEL), lambda b, t: (b * tiles + t, 0)),
        out_shape=jax.ShapeDtypeStruct((n_tok, D_MODEL), F32),
        scratch_shapes=[
            pltpu.VMEM((TILE_B, D_MODEL), BF16),
            pltpu.VMEM((TILE_B, D_MODEL), BF16),
            pltpu.VMEM((B_LEFT_PAD + TILE_B, B_WIDTH), BF16),
            pltpu.VMEM((B_WIDTH, B_LEFT_PAD + TILE_B), BF16),
            pltpu.VMEM((B_WIDTH, TILE_B), BF16),
            pltpu.VMEM((B_WIDTH, TILE_B), F32),
            pltpu.VMEM((B_WIDTH, TILE_B), BF16),
            pltpu.VMEM((N_GROUP, B_WIN, GROUP_COLS), F32),
        ],
        compiler_params=pltpu.CompilerParams(
            dimension_semantics=("arbitrary", "arbitrary"), vmem_limit_bytes=VMEM_LIMIT_BYTES),
        name="layer_b_attention",
    )(x2d, g_kv, g_b, w_k, w_vt, w_qt, w_zt, ext, w_out, g_f)


def kernel(x, a_norm_g, a_w_in, a_ln_g, a_ln_b, a_w_s, a_b_s, a_w_out, kv_norm_g, w_kv,
           b_norm_g, b_w_qz, b_rel_bias, b_w_out, final_norm_g):
    batch, seq, d = x.shape
    assert d == D_MODEL and seq % TILE_B == 0 and (batch * seq) % TILE_A == 0
    assert a_w_in.shape[0] == 1 and b_w_qz.shape[0] == 1
    assert b_rel_bias.shape[-1] == 2 * REL_CLIP + 1
    x2d = x.reshape(batch * seq, d)
    row = lambda v: v.reshape(1, -1).astype(F32)

    x1 = _layer_a(x2d, row(a_norm_g[0]), a_w_in[0].astype(BF16), row(a_ln_g[0]), row(a_ln_b[0]),
                  a_w_s[0], a_b_s[0].T, a_w_out[0].astype(BF16))
    w_qz = b_w_qz[0]
    ext = jnp.pad(b_rel_bias[0].astype(F32), ((0, 0), (0, EXT_LEN - b_rel_bias.shape[-1])),
                  mode="edge")
    out = _layer_b(x1, batch, row(kv_norm_g), row(b_norm_g[0]),
                   w_kv[:, :B_WIDTH].astype(BF16), w_kv[:, B_WIDTH:].T.astype(BF16),
                   w_qz[:, :B_WIDTH].T.astype(BF16), w_qz[:, B_WIDTH:].T.astype(BF16),
                   ext, b_w_out[0].astype(BF16), row(final_norm_g))
    return out.reshape(batch, seq, d)
```

```python
import jax
import jax.numpy as jnp
from jax import lax
from jax.experimental import pallas as pl
from jax.experimental.pallas import tpu as pltpu

D_MODEL = 1024
CHUNK = 64
A_WIDTH = 2 * D_MODEL
A_GROUP_LEN = 128
A_HEAD_CH = 128
A_HEADS = A_WIDTH // A_HEAD_CH
B_HEAD_DIM = 64
B_HEADS = D_MODEL // B_HEAD_DIM
B_WIDTH = B_HEADS * B_HEAD_DIM
B_PREV_CHUNKS = 8
B_LEFT_PAD = B_PREV_CHUNKS * CHUNK
REL_CLIP = 256
EPS = 1e-6
NEG_INF = -1e30
LOG2_E = 1.4426950408889634

LANES = 128
SUBLANES = 8
MXU_WIDTH = 256
VMEM_LIMIT_BYTES = 56 * 1024 * 1024

TILE_A = 256
TILE_B = B_LEFT_PAD
COL = MXU_WIDTH
UZ_AHEAD = 3

BF16 = jnp.bfloat16
F32 = jnp.float32


def _dot(a, b):
    return jnp.dot(a, b, preferred_element_type=F32)


def _dot_nt(a, b):
    return lax.dot_general(a, b, (((1,), (1,)), ((), ())), preferred_element_type=F32)


def _silu(v):
    half = 0.5 * v
    return half + half * jnp.tanh(half)


def _lane_fold(v):
    acc = v[:, :LANES]
    for i in range(1, v.shape[1] // LANES):
        acc = acc + v[:, i * LANES:(i + 1) * LANES]
    return acc


def _layer_a_kernel(x_ref, g_ref, win_ref, lng_ref, lnb_ref, ws_ref, bst_ref, wout_ref,
                    o_ref, h_s, gv_s, vn_s, y_s):
    n_col = A_WIDTH // COL
    n_grp = TILE_A // A_GROUP_LEN

    x = x_ref[...]
    ms = jnp.mean(x * x, axis=-1, keepdims=True)
    h_s[...] = (x * lax.rsqrt(ms + EPS) * g_ref[...]).astype(BF16)

    acc = jnp.zeros((TILE_A, LANES), F32)
    for c in range(n_col):
        cs = slice(c * COL, (c + 1) * COL)
        gv = jax.nn.gelu(_dot(h_s[...], win_ref[:, A_WIDTH + c * COL:A_WIDTH + (c + 1) * COL]))
        gv_s[:, cs] = gv
        acc = acc + _lane_fold(gv)
    def uz_dots(c):
        return (_dot(h_s[...], win_ref[:, c * COL:(c + 1) * COL]),
                _dot(h_s[...], win_ref[:, 2 * A_WIDTH + c * COL:2 * A_WIDTH + (c + 1) * COL]))

    uz = {c: uz_dots(c) for c in range(UZ_AHEAD)}

    mu = jnp.sum(acc, axis=-1, keepdims=True) * (1.0 / A_WIDTH)
    acc = jnp.zeros((TILE_A, LANES), F32)
    for c in range(n_col):
        d = gv_s[:, c * COL:(c + 1) * COL] - mu
        acc = acc + _lane_fold(d * d)
    rstd = lax.rsqrt(jnp.sum(acc, axis=-1, keepdims=True) * (1.0 / A_WIDTH) + EPS)
    for c in range(n_col):
        cs = slice(c * COL, (c + 1) * COL)
        vn = (gv_s[:, cs] - mu) * rstd * lng_ref[:, cs] + lnb_ref[:, cs]
        vn_s[:, cs] = vn.astype(BF16)

    t_idx = lax.broadcasted_iota(jnp.int32, (A_GROUP_LEN, A_GROUP_LEN), 0) // CHUNK
    s_idx = lax.broadcasted_iota(jnp.int32, (A_GROUP_LEN, A_GROUP_LEN), 1) // CHUNK
    allowed = t_idx >= s_idx

    for c in range(n_col):
        if c + UZ_AHEAD < n_col:
            uz[c + UZ_AHEAD] = uz_dots(c + UZ_AHEAD)
        u_raw, z_raw = uz.pop(c)
        u = jax.nn.gelu(u_raw)
        sz = _silu(z_raw)
        for hh in range(COL // A_HEAD_CH):
            hd = c * (COL // A_HEAD_CH) + hh
            hs = slice(hd * A_HEAD_CH, (hd + 1) * A_HEAD_CH)
            ls = slice(hh * A_HEAD_CH, (hh + 1) * A_HEAD_CH)
            wm = jnp.where(allowed, ws_ref[hd], 0.0).astype(BF16)
            bias = jnp.broadcast_to(bst_ref[:, hd:hd + 1], (A_GROUP_LEN, A_HEAD_CH))
            for g in range(n_grp):
                rs = slice(g * A_GROUP_LEN, (g + 1) * A_GROUP_LEN)
                mixed = _dot(wm, vn_s[rs, hs]) + bias
                y_s[rs, hs] = (u[rs, ls] * mixed * sz[rs, ls]).astype(BF16)

    o_ref[...] = x_ref[...] + _dot(y_s[...], wout_ref[...])


def _layer_a(x2d, g, w_in, ln_g, ln_b, w_s, b_st, w_out):
    n_tok = x2d.shape[0]
    const = lambda shape: pl.BlockSpec(shape, lambda i: (0,) * len(shape),
                                       pipeline_mode=pl.Buffered(1))
    return pl.pallas_call(
        _layer_a_kernel,
        grid=(n_tok // TILE_A,),
        in_specs=[
            pl.BlockSpec((TILE_A, D_MODEL), lambda i: (i, 0)),
            const((1, D_MODEL)),
            const((D_MODEL, 3 * A_WIDTH)),
            const((1, A_WIDTH)),
            const((1, A_WIDTH)),
            const((A_HEADS, A_GROUP_LEN, A_GROUP_LEN)),
            const((A_GROUP_LEN, A_HEADS)),
            const((A_WIDTH, D_MODEL)),
        ],
        out_specs=pl.BlockSpec((TILE_A, D_MODEL), lambda i: (i, 0)),
        out_shape=jax.ShapeDtypeStruct((n_tok, D_MODEL), F32),
        scratch_shapes=[
            pltpu.VMEM((TILE_A, D_MODEL), BF16),
            pltpu.VMEM((TILE_A, A_WIDTH), F32),
            pltpu.VMEM((TILE_A, A_WIDTH), BF16),
            pltpu.VMEM((TILE_A, A_WIDTH), BF16),
        ],
        compiler_params=pltpu.CompilerParams(
            dimension_semantics=("arbitrary",), vmem_limit_bytes=VMEM_LIMIT_BYTES),
        name="layer_a_gmlp",
    )(x2d, g, w_in, ln_g, ln_b, w_s, b_st, w_out)


B_PAIR = 2 * CHUNK
B_WIN = B_LEFT_PAD + B_PAIR
GROUP_HEADS = MXU_WIDTH // B_HEAD_DIM
GROUP_W = GROUP_HEADS * B_HEAD_DIM
N_GROUP = B_HEADS // GROUP_HEADS
GROUP_COLS = GROUP_HEADS * B_PAIR
EXT_LEN = 1024


def _fold_rows(v, op):
    parts = [v[r:r + SUBLANES] for r in range(0, v.shape[0], SUBLANES)]
    while len(parts) > 1:
        nxt = [op(parts[k], parts[k + 1]) for k in range(0, len(parts) - 1, 2)]
        if len(parts) % 2:
            nxt.append(parts[-1])
        parts = nxt
    return parts[0]


def _build_bias_table(ext_ref, bias_s):
    r_in = lax.broadcasted_iota(jnp.int32, (LANES, LANES), 0)
    c_in = lax.broadcasted_iota(jnp.int32, (LANES, LANES), 1)
    parity = c_in // CHUNK
    for jb in range(B_WIN // LANES):
        key_chunk = 2 * jb + r_in // CHUNK
        in_band = jnp.logical_and(key_chunk >= parity, key_chunk <= parity + B_PREV_CHUNKS)
        start = B_WIN - LANES * jb
        for h in range(B_HEADS):
            g, a = divmod(h, GROUP_HEADS)
            row = jnp.broadcast_to(ext_ref[h:h + 1, start:start + 2 * LANES], (LANES, 2 * LANES))
            toep = pltpu.roll(row, 0, 1, stride=1, stride_axis=0)[:, LANES:]
            bias_s[g, jb * LANES:(jb + 1) * LANES, a * LANES:(a + 1) * LANES] = jnp.where(
                in_band, toep * LOG2_E, NEG_INF)


def _layer_b_kernel(x_ref, gkv_ref, gb_ref, wk_ref, wvt_ref, wqt_ref, wzt_ref, ext_ref, wout_ref,
                    gf_ref, o_ref, hk_s, hq_s, k_s, vt_s, qt_s, szt_s, ogt_s, bias_s):
    b = pl.program_id(0)
    t = pl.program_id(1)
    cur = slice(B_LEFT_PAD, B_LEFT_PAD + TILE_B)

    @pl.when(jnp.logical_and(b == 0, t == 0))
    def _():
        _build_bias_table(ext_ref, bias_s)

    @pl.when(t == 0)
    def _():
        k_s[0:B_LEFT_PAD, :] = jnp.zeros((B_LEFT_PAD, B_WIDTH), BF16)
        vt_s[:, 0:B_LEFT_PAD] = jnp.zeros((B_WIDTH, B_LEFT_PAD), BF16)

    x = x_ref[...]
    ms = jnp.mean(x * x, axis=-1, keepdims=True)
    xn = x * lax.rsqrt(ms + EPS)
    hk_s[...] = (xn * gkv_ref[...]).astype(BF16)
    hq_s[...] = (xn * gb_ref[...]).astype(BF16)

    scale = B_HEAD_DIM ** -0.5 * LOG2_E

    def project(g, part):
        gs = slice(g * GROUP_W, (g + 1) * GROUP_W)
        if part == 0:
            k_s[cur, gs] = _dot(hk_s[...], wk_ref[:, gs]).astype(BF16)
        elif part == 1:
            qt_s[gs, :] = (_dot_nt(wqt_ref[gs, :], hq_s[...]) * scale).astype(BF16)
        elif part == 2:
            vt_s[gs, cur] = _dot_nt(wvt_ref[gs, :], hk_s[...]).astype(BF16)
        else:
            szt_s[gs, :] = _silu(_dot_nt(wzt_ref[gs, :], hq_s[...]))

    pen = jnp.where(t == 0, NEG_INF, 0.0).astype(F32)
    row_head = lax.broadcasted_iota(jnp.int32, (GROUP_W, B_PAIR), 0) // B_HEAD_DIM
    n_blk = B_WIN // LANES
    hist_blk = B_LEFT_PAD // LANES

    def scores(i, g):
        ts = slice(i * B_PAIR, (i + 1) * B_PAIR)
        ws = slice(i * B_PAIR, i * B_PAIR + B_WIN)
        gs = slice(g * GROUP_W, (g + 1) * GROUP_W)
        qt = qt_s[gs, ts]
        zero = jnp.zeros_like(qt)
        q4 = jnp.concatenate([jnp.where(row_head == a, qt, zero) for a in range(GROUP_HEADS)],
                             axis=1)
        return _dot(k_s[ws, gs], q4) + bias_s[g]

    def attend(i, g, s):
        ts = slice(i * B_PAIR, (i + 1) * B_PAIR)
        ws = slice(i * B_PAIR, i * B_PAIR + B_WIN)
        gs = slice(g * GROUP_W, (g + 1) * GROUP_W)
        blocks = [s[j * LANES:(j + 1) * LANES] for j in range(n_blk)]
        before_start = [i + j < hist_blk for j in range(n_blk)]
        m8 = None
        for j in range(n_blk):
            bm = _fold_rows(blocks[j], jnp.maximum)
            if before_start[j]:
                bm = bm + pen
            m8 = bm if m8 is None else jnp.maximum(m8, bm)
        m = jnp.max(m8, axis=0, keepdims=True)
        m_pen = m - pen
        l8 = None
        probs = []
        for j in range(n_blk):
            e = jnp.exp2(blocks[j] - (m_pen if before_start[j] else m))
            le = _fold_rows(e, jnp.add)
            l8 = le if l8 is None else l8 + le
            probs.append(e.astype(BF16))
        inv_l = 1.0 / jnp.sum(l8, axis=0, keepdims=True)
        p = jnp.concatenate(probs, axis=0)
        for a in range(GROUP_HEADS):
            rs = slice(g * GROUP_W + a * B_HEAD_DIM, g * GROUP_W + (a + 1) * B_HEAD_DIM)
            cs = slice(a * B_PAIR, (a + 1) * B_PAIR)
            o = _dot(vt_s[rs, ws], p[:, cs]) * inv_l[:, cs]
            ogt_s[rs, ts] = (o * szt_s[rs, ts]).astype(BF16)

    n_pair = TILE_B // B_PAIR
    assert n_pair == 4
    bodies = [(i, g) for g in range(N_GROUP) for i in range(n_pair)]
    for part in range(4):
        project(0, part)
    s_next = scores(*bodies[0])
    for n, (i, g) in enumerate(bodies):
        s = s_next
        if n + 1 < len(bodies):
            s_next = scores(*bodies[n + 1])
        if g + 1 < N_GROUP:
            project(g + 1, i)
        attend(i, g, s)

    y = x_ref[...] + _dot(ogt_s[...].T, wout_ref[...])
    ms = jnp.mean(y * y, axis=-1, keepdims=True)
    o_ref[...] = y * lax.rsqrt(ms + EPS) * gf_ref[...]

    k_s[0:B_LEFT_PAD, :] = k_s[cur, :]
    vt_s[:, 0:B_LEFT_PAD] = vt_s[:, cur]


def _layer_b(x2d, batch, g_kv, g_b, w_k, w_vt, w_qt, w_zt, ext, w_out, g_f):
    n_tok = x2d.shape[0]
    tiles = n_tok // batch // TILE_B
    const = lambda shape: pl.BlockSpec(shape, lambda b, t: (0,) * len(shape),
                                       pipeline_mode=pl.Buffered(1))
    return pl.pallas_call(
        _layer_b_kernel,
        grid=(batch, tiles),
        in_specs=[
            pl.BlockSpec((TILE_B, D_MODEL), lambda b, t: (b * tiles + t, 0)),
            const((1, D_MODEL)),
            const((1, D_MODEL)),
            const((D_MODEL, B_WIDTH)),
            const((B_WIDTH, D_MODEL)),
            const((B_WIDTH, D_MODEL)),
            const((B_WIDTH, D_MODEL)),
            const((B_HEADS, EXT_LEN)),
            const((B_WIDTH, D_MODEL)),
            const((1, D_MODEL)),
        ],
        out_specs=pl.BlockSpec((TILE_B, D_MODEL), lambda b, t: (b * tiles + t, 0)),
        out_shape=jax.ShapeDtypeStruct((n_tok, D_MODEL), F32),
        scratch_shapes=[
            pltpu.VMEM((TILE_B, D_MODEL), BF16),
            pltpu.VMEM((TILE_B, D_MODEL), BF16),
            pltpu.VMEM((B_LEFT_PAD + TILE_B, B_WIDTH), BF16),
            pltpu.VMEM((B_WIDTH, B_LEFT_PAD + TILE_B), BF16),
            pltpu.VMEM((B_WIDTH, TILE_B), BF16),
            pltpu.VMEM((B_WIDTH, TILE_B), F32),
            pltpu.VMEM((B_WIDTH, TILE_B), BF16),
            pltpu.VMEM((N_GROUP, B_WIN, GROUP_COLS), F32),
        ],
        compiler_params=pltpu.CompilerParams(
            dimension_semantics=("arbitrary", "arbitrary"), vmem_limit_bytes=VMEM_LIMIT_BYTES),
        name="layer_b_attention",
    )(x2d, g_kv, g_b, w_k, w_vt, w_qt, w_zt, ext, w_out, g_f)


def kernel(x, a_norm_g, a_w_in, a_ln_g, a_ln_b, a_w_s, a_b_s, a_w_out, kv_norm_g, w_kv,
           b_norm_g, b_w_qz, b_rel_bias, b_w_out, final_norm_g):
    batch, seq, d = x.shape
    assert d == D_MODEL and seq % TILE_B == 0 and (batch * seq) % TILE_A == 0
    assert a_w_in.shape[0] == 1 and b_w_qz.shape[0] == 1
    assert b_rel_bias.shape[-1] == 2 * REL_CLIP + 1
    x2d = x.reshape(batch * seq, d)
    row = lambda v: v.reshape(1, -1).astype(F32)

    x1 = _layer_a(x2d, row(a_norm_g[0]), a_w_in[0].astype(BF16), row(a_ln_g[0]), row(a_ln_b[0]),
                  a_w_s[0], a_b_s[0].T, a_w_out[0].astype(BF16))
    w_qz = b_w_qz[0]
    ext = jnp.pad(b_rel_bias[0].astype(F32), ((0, 0), (0, EXT_LEN - b_rel_bias.shape[-1])),
                  mode="edge")
    out = _layer_b(x1, batch, row(kv_norm_g), row(b_norm_g[0]),
                   w_kv[:, :B_WIDTH].astype(BF16), w_kv[:, B_WIDTH:].T.astype(BF16),
                   w_qz[:, :B_WIDTH].T.astype(BF16), w_qz[:, B_WIDTH:].T.astype(BF16),
                   ext, b_w_out[0].astype(BF16), row(final_norm_g))
    return out.reshape(batch, seq, d)
```

```python
import jax
import jax.numpy as jnp
from jax import lax
from jax.experimental import pallas as pl
from jax.experimental.pallas import tpu as pltpu

D_MODEL = 1024
CHUNK = 64
A_WIDTH = 2 * D_MODEL
A_GROUP_LEN = 128
A_HEAD_CH = 128
A_HEADS = A_WIDTH // A_HEAD_CH
B_HEAD_DIM = 64
B_HEADS = D_MODEL // B_HEAD_DIM
B_WIDTH = B_HEADS * B_HEAD_DIM
B_PREV_CHUNKS = 8
B_LEFT_PAD = B_PREV_CHUNKS * CHUNK
REL_CLIP = 256
EPS = 1e-6
NEG_INF = -1e30
LOG2_E = 1.4426950408889634

LANES = 128
SUBLANES = 8
MXU_WIDTH = 256
VMEM_LIMIT_BYTES = 56 * 1024 * 1024

TILE_A = 256
TILE_B = B_LEFT_PAD
COL = MXU_WIDTH
UZ_AHEAD = 3

BF16 = jnp.bfloat16
F32 = jnp.float32


def _dot(a, b):
    return jnp.dot(a, b, preferred_element_type=F32)


def _dot_nt(a, b):
    return lax.dot_general(a, b, (((1,), (1,)), ((), ())), preferred_element_type=F32)


def _silu(v):
    half = 0.5 * v
    return half + half * jnp.tanh(half)


def _lane_fold(v):
    acc = v[:, :LANES]
    for i in range(1, v.shape[1] // LANES):
        acc = acc + v[:, i * LANES:(i + 1) * LANES]
    return acc


def _layer_a_kernel(x_ref, g_ref, win_ref, lng_ref, lnb_ref, ws_ref, bst_ref, wout_ref,
                    o_ref, h_s, gv_s, vn_s, y_s):
    n_col = A_WIDTH // COL
    n_grp = TILE_A // A_GROUP_LEN

    x = x_ref[...]
    ms = jnp.mean(x * x, axis=-1, keepdims=True)
    h_s[...] = (x * lax.rsqrt(ms + EPS) * g_ref[...]).astype(BF16)

    acc = jnp.zeros((TILE_A, LANES), F32)
    for c in range(n_col):
        cs = slice(c * COL, (c + 1) * COL)
        gv = jax.nn.gelu(_dot(h_s[...], win_ref[:, A_WIDTH + c * COL:A_WIDTH + (c + 1) * COL]))
        gv_s[:, cs] = gv
        acc = acc + _lane_fold(gv)
    def uz_dots(c):
        return (_dot(h_s[...], win_ref[:, c * COL:(c + 1) * COL]),
                _dot(h_s[...], win_ref[:, 2 * A_WIDTH + c * COL:2 * A_WIDTH + (c + 1) * COL]))

    uz = {c: uz_dots(c) for c in range(UZ_AHEAD)}

    mu = jnp.sum(acc, axis=-1, keepdims=True) * (1.0 / A_WIDTH)
    acc = jnp.zeros((TILE_A, LANES), F32)
    for c in range(n_col):
        d = gv_s[:, c * COL:(c + 1) * COL] - mu
        acc = acc + _lane_fold(d * d)
    rstd = lax.rsqrt(jnp.sum(acc, axis=-1, keepdims=True) * (1.0 / A_WIDTH) + EPS)
    for c in range(n_col):
        cs = slice(c * COL, (c + 1) * COL)
        vn = (gv_s[:, cs] - mu) * rstd * lng_ref[:, cs] + lnb_ref[:, cs]
        vn_s[:, cs] = vn.astype(BF16)

    t_idx = lax.broadcasted_iota(jnp.int32, (A_GROUP_LEN, A_GROUP_LEN), 0) // CHUNK
    s_idx = lax.broadcasted_iota(jnp.int32, (A_GROUP_LEN, A_GROUP_LEN), 1) // CHUNK
    allowed = t_idx >= s_idx

    for c in range(n_col):
        if c + UZ_AHEAD < n_col:
            uz[c + UZ_AHEAD] = uz_dots(c + UZ_AHEAD)
        u_raw, z_raw = uz.pop(c)
        u = jax.nn.gelu(u_raw)
        sz = _silu(z_raw)
        for hh in range(COL // A_HEAD_CH):
            hd = c * (COL // A_HEAD_CH) + hh
            hs = slice(hd * A_HEAD_CH, (hd + 1) * A_HEAD_CH)
            ls = slice(hh * A_HEAD_CH, (hh + 1) * A_HEAD_CH)
            wm = jnp.where(allowed, ws_ref[hd], 0.0).astype(BF16)
            bias = jnp.broadcast_to(bst_ref[:, hd:hd + 1], (A_GROUP_LEN, A_HEAD_CH))
            for g in range(n_grp):
                rs = slice(g * A_GROUP_LEN, (g + 1) * A_GROUP_LEN)
                mixed = _dot(wm, vn_s[rs, hs]) + bias
                y_s[rs, hs] = (u[rs, ls] * mixed * sz[rs, ls]).astype(BF16)

    o_ref[...] = x_ref[...] + _dot(y_s[...], wout_ref[...])


def _layer_a(x2d, g, w_in, ln_g, ln_b, w_s, b_st, w_out):
    n_tok = x2d.shape[0]
    const = lambda shape: pl.BlockSpec(shape, lambda i: (0,) * len(shape),
                                       pipeline_mode=pl.Buffered(1))
    return pl.pallas_call(
        _layer_a_kernel,
        grid=(n_tok // TILE_A,),
        in_specs=[
            pl.BlockSpec((TILE_A, D_MODEL), lambda i: (i, 0)),
            const((1, D_MODEL)),
            const((D_MODEL, 3 * A_WIDTH)),
            const((1, A_WIDTH)),
            const((1, A_WIDTH)),
            const((A_HEADS, A_GROUP_LEN, A_GROUP_LEN)),
            const((A_GROUP_LEN, A_HEADS)),
            const((A_WIDTH, D_MODEL)),
        ],
        out_specs=pl.BlockSpec((TILE_A, D_MODEL), lambda i: (i, 0)),
        out_shape=jax.ShapeDtypeStruct((n_tok, D_MODEL), F32),
        scratch_shapes=[
            pltpu.VMEM((TILE_A, D_MODEL), BF16),
            pltpu.VMEM((TILE_A, A_WIDTH), F32),
            pltpu.VMEM((TILE_A, A_WIDTH), BF16),
            pltpu.VMEM((TILE_A, A_WIDTH), BF16),
        ],
        compiler_params=pltpu.CompilerParams(
            dimension_semantics=("arbitrary",), vmem_limit_bytes=VMEM_LIMIT_BYTES),
        name="layer_a_gmlp",
    )(x2d, g, w_in, ln_g, ln_b, w_s, b_st, w_out)


B_PAIR = 2 * CHUNK
B_WIN = B_LEFT_PAD + B_PAIR
GROUP_HEADS = MXU_WIDTH // B_HEAD_DIM
GROUP_W = GROUP_HEADS * B_HEAD_DIM
N_GROUP = B_HEADS // GROUP_HEADS
GROUP_COLS = GROUP_HEADS * B_PAIR
EXT_LEN = 1024


def _fold_rows(v, op):
    parts = [v[r:r + SUBLANES] for r in range(0, v.shape[0], SUBLANES)]
    while len(parts) > 1:
        nxt = [op(parts[k], parts[k + 1]) for k in range(0, len(parts) - 1, 2)]
        if len(parts) % 2:
            nxt.append(parts[-1])
        parts = nxt
    return parts[0]


def _build_bias_table(ext_ref, bias_s):
    r_in = lax.broadcasted_iota(jnp.int32, (LANES, LANES), 0)
    c_in = lax.broadcasted_iota(jnp.int32, (LANES, LANES), 1)
    parity = c_in // CHUNK
    for jb in range(B_WIN // LANES):
        key_chunk = 2 * jb + r_in // CHUNK
        in_band = jnp.logical_and(key_chunk >= parity, key_chunk <= parity + B_PREV_CHUNKS)
        start = B_WIN - LANES * jb
        for h in range(B_HEADS):
            g, a = divmod(h, GROUP_HEADS)
            row = jnp.broadcast_to(ext_ref[h:h + 1, start:start + 2 * LANES], (LANES, 2 * LANES))
            toep = pltpu.roll(row, 0, 1, stride=1, stride_axis=0)[:, LANES:]
            bias_s[g, jb * LANES:(jb + 1) * LANES, a * LANES:(a + 1) * LANES] = jnp.where(
                in_band, toep * LOG2_E, NEG_INF)


def _layer_b_kernel(x_ref, gkv_ref, gb_ref, wkv_ref, wqz_ref, ext_ref, wout_ref, gf_ref, o_ref,
                    hk_s, hq_s, k_s, vt_s, qt_s, szt_s, ogt_s, bias_s, wvt_s, wqt_s, wzt_s):
    b = pl.program_id(0)
    t = pl.program_id(1)
    cur = slice(B_LEFT_PAD, B_LEFT_PAD + TILE_B)

    @pl.when(jnp.logical_and(b == 0, t == 0))
    def _():
        _build_bias_table(ext_ref, bias_s)
        wvt_s[...] = wkv_ref[:, B_WIDTH:].T
        wqt_s[...] = wqz_ref[:, :B_WIDTH].T
        wzt_s[...] = wqz_ref[:, B_WIDTH:].T

    @pl.when(t == 0)
    def _():
        k_s[0:B_LEFT_PAD, :] = jnp.zeros((B_LEFT_PAD, B_WIDTH), BF16)
        vt_s[:, 0:B_LEFT_PAD] = jnp.zeros((B_WIDTH, B_LEFT_PAD), BF16)

    x = x_ref[...]
    ms = jnp.mean(x * x, axis=-1, keepdims=True)
    xn = x * lax.rsqrt(ms + EPS)
    hk_s[...] = (xn * gkv_ref[...]).astype(BF16)
    hq_s[...] = (xn * gb_ref[...]).astype(BF16)

    scale = B_HEAD_DIM ** -0.5 * LOG2_E

    def project(g, part):
        gs = slice(g * GROUP_W, (g + 1) * GROUP_W)
        if part == 0:
            k_s[cur, gs] = _dot(hk_s[...], wkv_ref[:, gs]).astype(BF16)
        elif part == 1:
            qt_s[gs, :] = (_dot_nt(wqt_s[gs, :], hq_s[...]) * scale).astype(BF16)
        elif part == 2:
            vt_s[gs, cur] = _dot_nt(wvt_s[gs, :], hk_s[...]).astype(BF16)
        else:
            szt_s[gs, :] = _silu(_dot_nt(wzt_s[gs, :], hq_s[...]))

    pen = jnp.where(t == 0, NEG_INF, 0.0).astype(F32)
    row_head = lax.broadcasted_iota(jnp.int32, (GROUP_W, B_PAIR), 0) // B_HEAD_DIM
    n_blk = B_WIN // LANES
    hist_blk = B_LEFT_PAD // LANES

    def scores(i, g):
        ts = slice(i * B_PAIR, (i + 1) * B_PAIR)
        ws = slice(i * B_PAIR, i * B_PAIR + B_WIN)
        gs = slice(g * GROUP_W, (g + 1) * GROUP_W)
        qt = qt_s[gs, ts]
        zero = jnp.zeros_like(qt)
        q4 = jnp.concatenate([jnp.where(row_head == a, qt, zero) for a in range(GROUP_HEADS)],
                             axis=1)
        return _dot(k_s[ws, gs], q4) + bias_s[g]

    def attend(i, g, s):
        ts = slice(i * B_PAIR, (i + 1) * B_PAIR)
        ws = slice(i * B_PAIR, i * B_PAIR + B_WIN)
        gs = slice(g * GROUP_W, (g + 1) * GROUP_W)
        blocks = [s[j * LANES:(j + 1) * LANES] for j in range(n_blk)]
        before_start = [i + j < hist_blk for j in range(n_blk)]
        m8 = None
        for j in range(n_blk):
            bm = _fold_rows(blocks[j], jnp.maximum)
            if before_start[j]:
                bm = bm + pen
            m8 = bm if m8 is None else jnp.maximum(m8, bm)
        m = jnp.max(m8, axis=0, keepdims=True)
        m_pen = m - pen
        l8 = None
        probs = []
        for j in range(n_blk):
            e = jnp.exp2(blocks[j] - (m_pen if before_start[j] else m))
            le = _fold_rows(e, jnp.add)
            l8 = le if l8 is None else l8 + le
            probs.append(e.astype(BF16))
        inv_l = 1.0 / jnp.sum(l8, axis=0, keepdims=True)
        p = jnp.concatenate(probs, axis=0)
        for a in range(GROUP_HEADS):
            rs = slice(g * GROUP_W + a * B_HEAD_DIM, g * GROUP_W + (a + 1) * B_HEAD_DIM)
            cs = slice(a * B_PAIR, (a + 1) * B_PAIR)
            o = _dot(vt_s[rs, ws], p[:, cs]) * inv_l[:, cs]
            ogt_s[rs, ts] = (o * szt_s[rs, ts]).astype(BF16)

    n_pair = TILE_B // B_PAIR
    assert n_pair == 4
    bodies = [(i, g) for g in range(N_GROUP) for i in range(n_pair)]
    for part in range(4):
        project(0, part)
    s_next = scores(*bodies[0])
    for n, (i, g) in enumerate(bodies):
        s = s_next
        if n + 1 < len(bodies):
            s_next = scores(*bodies[n + 1])
        if g + 1 < N_GROUP:
            project(g + 1, i)
        attend(i, g, s)

    y = x_ref[...] + _dot(ogt_s[...].T, wout_ref[...])
    ms = jnp.mean(y * y, axis=-1, keepdims=True)
    o_ref[...] = y * lax.rsqrt(ms + EPS) * gf_ref[...]

    k_s[0:B_LEFT_PAD, :] = k_s[cur, :]
    vt_s[:, 0:B_LEFT_PAD] = vt_s[:, cur]


def _layer_b(x2d, batch, g_kv, g_b, w_kv, w_qz, ext, w_out, g_f):
    n_tok = x2d.shape[0]
    tiles = n_tok // batch // TILE_B
    const = lambda shape: pl.BlockSpec(shape, lambda b, t: (0,) * len(shape),
                                       pipeline_mode=pl.Buffered(1))
    return pl.pallas_call(
        _layer_b_kernel,
        grid=(batch, tiles),
        in_specs=[
            pl.BlockSpec((TILE_B, D_MODEL), lambda b, t: (b * tiles + t, 0)),
            const((1, D_MODEL)),
            const((1, D_MODEL)),
            const((D_MODEL, 2 * B_WIDTH)),
            const((D_MODEL, 2 * B_WIDTH)),
            const((B_HEADS, EXT_LEN)),
            const((B_WIDTH, D_MODEL)),
            const((1, D_MODEL)),
        ],
        out_specs=pl.BlockSpec((TILE_B, D_MODEL), lambda b, t: (b * tiles + t, 0)),
        out_shape=jax.ShapeDtypeStruct((n_tok, D_MODEL), F32),
        scratch_shapes=[
            pltpu.VMEM((TILE_B, D_MODEL), BF16),
            pltpu.VMEM((TILE_B, D_MODEL), BF16),
            pltpu.VMEM((B_LEFT_PAD + TILE_B, B_WIDTH), BF16),
            pltpu.VMEM((B_WIDTH, B_LEFT_PAD + TILE_B), BF16),
            pltpu.VMEM((B_WIDTH, TILE_B), BF16),
            pltpu.VMEM((B_WIDTH, TILE_B), F32),
            pltpu.VMEM((B_WIDTH, TILE_B), BF16),
            pltpu.VMEM((N_GROUP, B_WIN, GROUP_COLS), F32),
            pltpu.VMEM((B_WIDTH, D_MODEL), BF16),
            pltpu.VMEM((B_WIDTH, D_MODEL), BF16),
            pltpu.VMEM((B_WIDTH, D_MODEL), BF16),
        ],
        compiler_params=pltpu.CompilerParams(
            dimension_semantics=("arbitrary", "arbitrary"), vmem_limit_bytes=VMEM_LIMIT_BYTES),
        name="layer_b_attention",
    )(x2d, g_kv, g_b, w_kv, w_qz, ext, w_out, g_f)


def kernel(x, a_norm_g, a_w_in, a_ln_g, a_ln_b, a_w_s, a_b_s, a_w_out, kv_norm_g, w_kv,
           b_norm_g, b_w_qz, b_rel_bias, b_w_out, final_norm_g):
    batch, seq, d = x.shape
    assert d == D_MODEL and seq % TILE_B == 0 and (batch * seq) % TILE_A == 0
    assert a_w_in.shape[0] == 1 and b_w_qz.shape[0] == 1
    assert b_rel_bias.shape[-1] == 2 * REL_CLIP + 1
    x2d = x.reshape(batch * seq, d)
    row = lambda v: v.reshape(1, -1).astype(F32)

    x1 = _layer_a(x2d, row(a_norm_g[0]), a_w_in[0].astype(BF16), row(a_ln_g[0]), row(a_ln_b[0]),
                  a_w_s[0], a_b_s[0].T, a_w_out[0].astype(BF16))
    ext = jnp.pad(b_rel_bias[0].astype(F32), ((0, 0), (0, EXT_LEN - b_rel_bias.shape[-1])),
                  mode="edge")
    out = _layer_b(x1, batch, row(kv_norm_g), row(b_norm_g[0]), w_kv.astype(BF16),
                   b_w_qz[0].astype(BF16), ext, b_w_out[0].astype(BF16), row(final_norm_g))
    return out.reshape(batch, seq, d)
```
